```python
import math
import jax, jax.numpy as jnp
from jax import lax
import numpy as np

D_MODEL = 2048
BATCH = 4
SEQ = 4096
DEPTH = 4

N_META = 16
BLOCK = 128
WINDOW = 128
PAD_FRONT = BLOCK - N_META
MIX_WIDTH = D_MODEL
ATT_HEAD_DIM = 128
ATT_WIDTH = MIX_WIDTH // 2
ATT_HEADS = ATT_WIDTH // ATT_HEAD_DIM
ATT_KV_HEADS = 2
KV_WIDTH = ATT_KV_HEADS * ATT_HEAD_DIM
ROT_DIM = ATT_HEAD_DIM // 4
ROPE_THETA = 500000.0
RET_WIDTH = MIX_WIDTH - ATT_WIDTH
RET_HEAD_DIM = 256
RET_HEADS = RET_WIDTH // RET_HEAD_DIM
RET_THETA = 10000.0
D_FF = -(-8 * D_MODEL // (3 * 256)) * 256
SPLITS = [ATT_WIDTH, KV_WIDTH, KV_WIDTH, RET_WIDTH, RET_WIDTH, RET_WIDTH, RET_WIDTH]
SPLIT_IDX = [int(s) for s in np.cumsum(SPLITS)[:-1]]
IN_COLS = int(sum(SPLITS))
EPS = 1e-6
NEG = -1e30

kernel_name = "hymba_style_swa_retention_encoder"


def rms_norm(x, g):
    xf = x.astype(jnp.float32)
    y = xf * lax.rsqrt(jnp.mean(xf * xf, axis=-1, keepdims=True) + EPS)
    return (y * g.astype(jnp.float32)).astype(x.dtype)


def rope(x, pos, theta, rot_dim):
    half = rot_dim // 2
    inv = theta ** (-jnp.arange(half, dtype=jnp.float32) / half)
    ang = pos.astype(jnp.float32)[:, None] * inv[None, :]
    cos = jnp.cos(ang).astype(x.dtype)
    sin = jnp.sin(ang).astype(x.dtype)
    x1, x2, rest = x[..., :half], x[..., half:rot_dim], x[..., rot_dim:]
    return jnp.concatenate([x1 * cos - x2 * sin, x2 * cos + x1 * sin, rest], axis=-1)


def band_blocks(t):
    b_, h_, n_, d_ = t.shape
    nb = n_ // BLOCK
    tb = t.reshape(b_, h_, nb, BLOCK, d_)
    z = jnp.zeros_like(tb[:, :, :1])
    prev = jnp.concatenate([z, tb[:, :, :-1]], axis=2)
    nxt = jnp.concatenate([tb[:, :, 1:], z], axis=2)
    return jnp.concatenate([prev, tb, nxt], axis=3)


def windowed_sink_attention(q, k, v, sink):
    b_, hq, n_, dh = q.shape
    hkv = k.shape[1]
    g_ = hq // hkv
    nb = n_ // BLOCK
    qb = q.reshape(b_, hkv, g_, nb, BLOCK, dh)
    kb, vb = band_blocks(k), band_blocks(v)
    km, vm = k[:, :, PAD_FRONT:BLOCK], v[:, :, PAD_FRONT:BLOCK]
    qi = jnp.arange(nb)[:, None] * BLOCK + jnp.arange(BLOCK)[None, :]
    kj = (jnp.arange(nb)[:, None] - 1) * BLOCK + jnp.arange(3 * BLOCK)[None, :]
    band_ok = ((jnp.abs(qi[:, :, None] - kj[:, None, :]) <= WINDOW)
               & (kj[:, None, :] >= PAD_FRONT) & (kj[:, None, :] < n_))
    mj = PAD_FRONT + jnp.arange(N_META)
    meta_ok = jnp.abs(qi[:, :, None] - mj[None, None, :]) > WINDOW
    scale = dh ** -0.5
    s_band = jnp.einsum('bkgnqd,bknjd->bkgnqj', qb, kb).astype(jnp.float32) * scale
    s_meta = jnp.einsum('bkgnqd,bkmd->bkgnqm', qb, km).astype(jnp.float32) * scale
    s_sink = jnp.broadcast_to(sink.astype(jnp.float32).reshape(hkv, g_)[None, :, :, None, None, None],
                              s_band.shape[:-1] + (1,))
    s = jnp.concatenate([jnp.where(band_ok, s_band, NEG), jnp.where(meta_ok, s_meta, NEG), s_sink], axis=-1)
    p = jax.nn.softmax(s, axis=-1).astype(v.dtype)
    nk = 3 * BLOCK
    o = (jnp.einsum('bkgnqj,bknjd->bkgnqd', p[..., :nk], vb)
         + jnp.einsum('bkgnqm,bkmd->bkgnqd', p[..., nk:nk + N_META], vm))
    return o.reshape(b_, hq, n_, dh)


def retention_direction(q, k, v, log_gamma, include_diag):
    b_, h_, n_, dk = q.shape
    dv = v.shape[-1]
    nc = n_ // BLOCK
    qc = q.reshape(b_, h_, nc, BLOCK, dk)
    kc = k.reshape(b_, h_, nc, BLOCK, dk)
    vc = v.reshape(b_, h_, nc, BLOCK, dv)
    lg = log_gamma[:, None]
    idx = jnp.arange(BLOCK, dtype=jnp.float32)
    diff = idx[:, None] - idx[None, :]
    keep = (diff >= 0) if include_diag else (diff > 0)
    dmask = jnp.where(keep[None], jnp.exp(lg[:, :, None] * jnp.maximum(diff, 0.0)[None]), 0.0)
    scores = jnp.einsum('bhcid,bhcjd->bhcij', qc, kc) * dmask[None, :, None]
    intra = jnp.einsum('bhcij,bhcjv->bhciv', scores, vc)
    zeta = jnp.exp(lg * (BLOCK - 1 - idx)[None, :])
    kv_chunk = jnp.einsum('bhcjd,hj,bhcjv->bhcdv', kc, zeta, vc)
    decay_chunk = jnp.exp(log_gamma * BLOCK)[None, :, None, None]

    def step(state, kv_c):
        return state * decay_chunk + kv_c, state

    _, states = lax.scan(step, jnp.zeros((b_, h_, dk, dv), jnp.float32), jnp.moveaxis(kv_chunk, 2, 0))
    states = jnp.moveaxis(states, 0, 2)
    xi = jnp.exp(lg * (idx + 1.0)[None, :])
    cross = jnp.einsum('bhcid,hi,bhcdv->bhciv', qc, xi, states)
    return (intra + cross).reshape(b_, h_, n_, dv)


def bidirectional_retention(q, k, v, raw_fwd, raw_bwd):
    lg_f = -jnp.exp(raw_fwd.astype(jnp.float32))
    lg_b = -jnp.exp(raw_bwd.astype(jnp.float32))
    qf, kf, vf = q.astype(jnp.float32), k.astype(jnp.float32), v.astype(jnp.float32)
    fwd = retention_direction(qf, kf, vf, lg_f, True)
    bwd = retention_direction(qf[:, :, ::-1], kf[:, :, ::-1], vf[:, :, ::-1], lg_b, False)[:, :, ::-1]
    return fwd + bwd


def to_heads(t, n_heads, dh):
    b_, n_, _ = t.shape
    return t.reshape(b_, n_, n_heads, dh).transpose(0, 2, 1, 3)


def setup_inputs(seed: int = 0) -> dict:
    key = jax.random.key(seed)
    ks = jax.random.split(key, 16)
    f32 = jnp.float32

    def nrm(k, shape, scale):
        return jax.random.normal(k, shape, f32) * scale

    def gain(k, shape):
        return 1.0 + 0.02 * jax.random.normal(k, shape, f32)

    base_decay = np.log(-np.log(1.0 - 2.0 ** (-5.0 - np.arange(RET_HEADS)))).astype(np.float32)
    return {
        "x": nrm(ks[0], (BATCH, SEQ, D_MODEL), 1.0),
        "meta_tokens": nrm(ks[1], (N_META, D_MODEL), 1.0),
        "w_in": nrm(ks[2], (DEPTH, D_MODEL, IN_COLS), D_MODEL ** -0.5),
        "w_out": nrm(ks[3], (DEPTH, MIX_WIDTH, D_MODEL), MIX_WIDTH ** -0.5),
        "attn_sink": nrm(ks[4], (DEPTH, ATT_HEADS), 0.5),
        "ret_decay_fwd": jnp.asarray(base_decay)[None, :] + nrm(ks[5], (DEPTH, RET_HEADS), 0.01),
        "ret_decay_bwd": jnp.asarray(base_decay)[None, :] + nrm(ks[6], (DEPTH, RET_HEADS), 0.01),
        "ret_norm": gain(ks[7], (DEPTH, RET_WIDTH)),
        "norm_mix_pre": gain(ks[8], (DEPTH, D_MODEL)),
        "norm_mix_post": gain(ks[9], (DEPTH, D_MODEL)),
        "w_gate": nrm(ks[10], (DEPTH, D_MODEL, D_FF), D_MODEL ** -0.5),
        "w_up": nrm(ks[11], (DEPTH, D_MODEL, D_FF), D_MODEL ** -0.5),
        "w_down": nrm(ks[12], (DEPTH, D_FF, D_MODEL), D_FF ** -0.5),
        "norm_ffn_pre": gain(ks[13], (DEPTH, D_MODEL)),
        "norm_ffn_post": gain(ks[14], (DEPTH, D_MODEL)),
    }


def reference(x, meta_tokens, w_in, w_out, attn_sink, ret_decay_fwd, ret_decay_bwd, ret_norm,
              norm_mix_pre, norm_mix_post, w_gate, w_up, w_down, norm_ffn_pre, norm_ffn_post):
    b_ = x.shape[0]
    meta = jnp.broadcast_to(meta_tokens.astype(x.dtype)[None], (b_, N_META, D_MODEL))
    h = jnp.concatenate([meta, x], axis=1)
    n_pad = PAD_FRONT + h.shape[1]
    pos = jnp.arange(n_pad) - PAD_FRONT
    for l in range(DEPTH):
        u = rms_norm(h, norm_mix_pre[l])
        proj = jnp.einsum('bnd,dc->bnc', u, w_in[l])
        proj = jnp.pad(proj, ((0, 0), (PAD_FRONT, 0), (0, 0)))
        aq, ak, av, rq, rk, rv, rg = jnp.split(proj, SPLIT_IDX, axis=-1)
        aq = rope(to_heads(aq, ATT_HEADS, ATT_HEAD_DIM), pos, ROPE_THETA, ROT_DIM)
        ak = rope(to_heads(ak, ATT_KV_HEADS, ATT_HEAD_DIM), pos, ROPE_THETA, ROT_DIM)
        av = to_heads(av, ATT_KV_HEADS, ATT_HEAD_DIM)
        att = windowed_sink_attention(aq, ak, av, attn_sink[l])
        att = att.transpose(0, 2, 1, 3).reshape(b_, n_pad, ATT_WIDTH)
        rq = rope(to_heads(rq, RET_HEADS, RET_HEAD_DIM), pos, RET_THETA, RET_HEAD_DIM)
        rk = rope(to_heads(rk, RET_HEADS, RET_HEAD_DIM), pos, RET_THETA, RET_HEAD_DIM) * (RET_HEAD_DIM ** -0.5)
        rv = to_heads(rv, RET_HEADS, RET_HEAD_DIM)
        ret = bidirectional_retention(rq, rk, rv, ret_decay_fwd[l], ret_decay_bwd[l])
        ret = rms_norm(ret.transpose(0, 2, 1, 3), ret_norm[l].reshape(RET_HEADS, RET_HEAD_DIM))
        ret = ret.reshape(b_, n_pad, RET_WIDTH).astype(h.dtype)
        ret = jax.nn.silu(rg) * ret
        mixed = jnp.concatenate([att, ret], axis=-1)[:, PAD_FRONT:]
        mixed = jnp.einsum('bnc,cd->bnd', mixed, w_out[l])
        h = h + rms_norm(mixed, norm_mix_post[l])
        u = rms_norm(h, norm_ffn_pre[l])
        f = jax.nn.silu(jnp.einsum('bnd,df->bnf', u, w_gate[l])) * jnp.einsum('bnd,df->bnf', u, w_up[l])
        f = jnp.einsum('bnf,fd->bnd', f, w_down[l])
        h = h + rms_norm(f, norm_ffn_post[l])
    return h[:, N_META:]
```

```python
import functools

import jax
import jax.numpy as jnp
from jax import lax
from jax.experimental import pallas as pl
from jax.experimental.pallas import tpu as pltpu

F32 = jnp.float32
BF16 = jnp.bfloat16

D_MODEL = 2048
BATCH = 4
SEQ = 4096
DEPTH = 4
N_META = 16
BLOCK = 128
WINDOW = 128
PAD_FRONT = BLOCK - N_META
NP = PAD_FRONT + N_META + SEQ
NB = NP // BLOCK
M_ROWS = BATCH * NP
ATT_HEAD_DIM = 128
ATT_WIDTH = 1024
ATT_HEADS = 8
ATT_KV_HEADS = 2
ATT_GROUP = ATT_HEADS // ATT_KV_HEADS
KV_WIDTH = ATT_KV_HEADS * ATT_HEAD_DIM
ROT_DIM = 32
ROPE_THETA = 500000.0
RET_WIDTH = 1024
RET_HEAD_DIM = 256
RET_HEADS = 4
RET_THETA = 10000.0
D_FF = 5632
IN_COLS = ATT_WIDTH + 2 * KV_WIDTH + 4 * RET_WIDTH
EPS = 1e-6
NEG = -1e30

COL_AQ = 0
COL_AK = ATT_WIDTH
COL_AV = COL_AK + KV_WIDTH
COL_RQ = COL_AV + KV_WIDTH
COL_RK = COL_RQ + RET_WIDTH
COL_RV = COL_RK + RET_WIDTH
COL_RG = COL_RV + RET_WIDTH

VMEM_LIMIT = 56 * 1024 * 1024

TM_IN = 1408
TN_IN = 512
TM_OUT = 768
TM_FFN = 512
TF_FFN = 512


def _rms(x, g):
    return x * lax.rsqrt(jnp.mean(x * x, axis=-1, keepdims=True) + EPS) * g


def _valid_rows(row0, rows):
    r = row0 + lax.broadcasted_iota(jnp.int32, (rows, 1), 0)
    ok = r >= 0
    for b in range(BATCH):
        ok = ok & ~((r >= b * NP) & (r < b * NP + PAD_FRONT))
    return ok


def _embed_kernel(x_ref, meta_ref, g_ref, h_ref, u_ref):
    n = pl.program_id(1)

    @pl.when(n == 0)
    def _():
        h_ref[0:PAD_FRONT, :] = jnp.zeros((PAD_FRONT, D_MODEL), F32)
        u_ref[0:PAD_FRONT, :] = jnp.zeros((PAD_FRONT, D_MODEL), BF16)
        m = meta_ref[...]
        h_ref[PAD_FRONT:BLOCK, :] = m
        u_ref[PAD_FRONT:BLOCK, :] = _rms(m, g_ref[...]).astype(BF16)

    @pl.when(n > 0)
    def _():
        xv = x_ref[0]
        h_ref[...] = xv
        u_ref[...] = _rms(xv, g_ref[...]).astype(BF16)


def _embed(x, meta, g):
    return pl.pallas_call(
        _embed_kernel,
        grid=(BATCH, NB),
        in_specs=[
            pl.BlockSpec((1, BLOCK, D_MODEL), lambda b, n: (b, jnp.maximum(n - 1, 0), 0)),
            pl.BlockSpec((N_META, D_MODEL), lambda b, n: (0, 0)),
            pl.BlockSpec((1, D_MODEL), lambda b, n: (0, 0)),
        ],
        out_specs=[
            pl.BlockSpec((BLOCK, D_MODEL), lambda b, n: (b * NB + n, 0)),
            pl.BlockSpec((BLOCK, D_MODEL), lambda b, n: (b * NB + n, 0)),
        ],
        out_shape=[
            jax.ShapeDtypeStruct((M_ROWS, D_MODEL), F32),
            jax.ShapeDtypeStruct((M_ROWS, D_MODEL), BF16),
        ],
        compiler_params=pltpu.CompilerParams(
            dimension_semantics=("arbitrary", "arbitrary"),
            vmem_limit_bytes=VMEM_LIMIT),
        name="embed",
    )(x, meta, g)


def _inproj_kernel(u_ref, w_ref, ca_ref, s1_ref, s2_ref, cr_ref, sr_ref, o_ref):
    j = pl.program_id(1)
    acc = jnp.dot(u_ref[...], w_ref[...], preferred_element_type=F32)

    def rope_att(x):
        return (x * ca_ref[...]
                + pltpu.roll(x, ROT_DIM // 2, 1) * s1_ref[...]
                + pltpu.roll(x, ATT_HEAD_DIM - ROT_DIM // 2, 1) * s2_ref[...])

    def rope_ret(x, scale):
        x1 = x[:, :128]
        x2 = x[:, 128:]
        c = cr_ref[...]
        s = sr_ref[...]
        y1 = x1 * c - x2 * s
        y2 = x2 * c + x1 * s
        if scale != 1.0:
            y1 = y1 * scale
            y2 = y2 * scale
        return y1, y2

    n_aq = ATT_WIDTH // TN_IN
    j_akv = n_aq
    j_rq = COL_RQ // TN_IN
    j_rk = COL_RK // TN_IN
    j_rv = COL_RV // TN_IN

    @pl.when(j < n_aq)
    def _():
        for hh in range(TN_IN // ATT_HEAD_DIM):
            sl = slice(hh * ATT_HEAD_DIM, (hh + 1) * ATT_HEAD_DIM)
            o_ref[:, sl] = rope_att(acc[:, sl]).astype(BF16)

    @pl.when(j == j_akv)
    def _():
        for hh in range(ATT_KV_HEADS):
            sl = slice(hh * ATT_HEAD_DIM, (hh + 1) * ATT_HEAD_DIM)
            o_ref[:, sl] = rope_att(acc[:, sl]).astype(BF16)
        o_ref[:, KV_WIDTH:] = acc[:, KV_WIDTH:].astype(BF16)

    def ret_tiles(scale):
        for hh in range(TN_IN // RET_HEAD_DIM):
            base = hh * RET_HEAD_DIM
            y1, y2 = rope_ret(acc[:, base:base + RET_HEAD_DIM], scale)
            o_ref[:, base:base + 128] = y1.astype(BF16)
            o_ref[:, base + 128:base + 256] = y2.astype(BF16)

    @pl.when((j >= j_rq) & (j < j_rk))
    def _():
        ret_tiles(1.0)

    @pl.when((j >= j_rk) & (j < j_rv))
    def _():
        ret_tiles(RET_HEAD_DIM ** -0.5)

    @pl.when(j >= j_rv)
    def _():
        o_ref[...] = acc.astype(BF16)


def _inproj(u, w, tabs):
    tiles_per_seq = NP // TM_IN
    tab_spec = pl.BlockSpec((TM_IN, 128), lambda i, j: (i % tiles_per_seq, 0))
    return pl.pallas_call(
        _inproj_kernel,
        grid=(M_ROWS // TM_IN, IN_COLS // TN_IN),
        in_specs=[
            pl.BlockSpec((TM_IN, D_MODEL), lambda i, j: (i, 0)),
            pl.BlockSpec((D_MODEL, TN_IN), lambda i, j: (0, j)),
            tab_spec, tab_spec, tab_spec, tab_spec, tab_spec,
        ],
        out_specs=pl.BlockSpec((TM_IN, TN_IN), lambda i, j: (i, j)),
        out_shape=jax.ShapeDtypeStruct((M_ROWS, IN_COLS), BF16),
        compiler_params=pltpu.CompilerParams(
            dimension_semantics=("arbitrary", "arbitrary"),
            vmem_limit_bytes=VMEM_LIMIT),
        name="inproj",
    )(u, w, *tabs)


def _att_mask(n):
    row = lax.broadcasted_iota(jnp.int32, (BLOCK, 4 * BLOCK), 0)
    col = lax.broadcasted_iota(jnp.int32, (BLOCK, 4 * BLOCK), 1)
    qi = n * BLOCK + row
    kj = (n - 1) * BLOCK + col
    is_band = col < 3 * BLOCK
    band = is_band & (jnp.abs(qi - kj) <= WINDOW) & (kj >= PAD_FRONT) & (kj < NP)
    mj = col - 3 * BLOCK
    meta = (~is_band) & (mj >= PAD_FRONT) & (jnp.abs(qi - mj) > WINDOW)
    return band | meta


def _attn_kernel(sink_ref, q_ref, k_ref, v_ref, o_ref, bias_ref):
    kv = pl.program_id(1)
    scale = ATT_HEAD_DIM ** -0.5
    rows = ATT_GROUP * BLOCK

    head = lax.broadcasted_iota(jnp.int32, (rows, 1), 0) // BLOCK
    sink = jnp.zeros((rows, 1), F32)
    for g in range(ATT_GROUP):
        sink = jnp.where(head == g, sink_ref[kv * ATT_GROUP + g], sink)

    k_meta = k_ref[0:BLOCK, :]
    v_meta = v_ref[0:BLOCK, :]

    bias_ref[...] = jnp.where(_att_mask(2), 0.0, NEG).astype(F32)

    def block(n, bias):
        r_own = pl.ds(pl.multiple_of(n * BLOCK, BLOCK), BLOCK)
        r_prev = pl.ds(pl.multiple_of(jnp.maximum(n - 1, 0) * BLOCK, BLOCK), BLOCK)
        r_next = pl.ds(pl.multiple_of(jnp.minimum(n + 1, NB - 1) * BLOCK, BLOCK), BLOCK)
        q = q_ref[r_own, :]
        qs = jnp.concatenate(
            [q[:, g * ATT_HEAD_DIM:(g + 1) * ATT_HEAD_DIM] for g in range(ATT_GROUP)], axis=0)
        ks = jnp.concatenate([k_ref[r_prev, :], k_ref[r_own, :], k_ref[r_next, :], k_meta], axis=0)
        vs = jnp.concatenate([v_ref[r_prev, :], v_ref[r_own, :], v_ref[r_next, :], v_meta], axis=0)
        s = lax.dot_general(qs, ks, (((1,), (1,)), ((), ())), preferred_element_type=F32)
        s = s * scale + jnp.concatenate([bias] * ATT_GROUP, axis=0)
        m = jnp.maximum(jnp.max(s, axis=-1, keepdims=True), sink)
        p = jnp.exp(s - m)
        den = jnp.sum(p, axis=-1, keepdims=True) + jnp.exp(sink - m)
        o = jnp.dot(p.astype(BF16), vs, preferred_element_type=F32) / den
        for g in range(ATT_GROUP):
            o_ref[r_own, g * ATT_HEAD_DIM:(g + 1) * ATT_HEAD_DIM] = (
                o[g * BLOCK:(g + 1) * BLOCK, :].astype(BF16))

    def edge(n):
        block(n, jnp.where(_att_mask(n), 0.0, NEG).astype(F32))

    edge(0)
    edge(1)

    def body(n, carry):
        block(n, bias_ref[...])
        return carry

    lax.fori_loop(2, NB - 1, body, 0)
    edge(NB - 1)


def _attention(proj, sink):
    q_blk = ATT_GROUP * ATT_HEAD_DIM
    return pl.pallas_call(
        _attn_kernel,
        grid=(BATCH, ATT_KV_HEADS),
        in_specs=[
            pl.BlockSpec(memory_space=pltpu.SMEM),
            pl.BlockSpec((NP, q_blk), lambda b, k: (b, k)),
            pl.BlockSpec((NP, ATT_HEAD_DIM), lambda b, k: (b, COL_AK // ATT_HEAD_DIM + k)),
            pl.BlockSpec((NP, ATT_HEAD_DIM), lambda b, k: (b, COL_AV // ATT_HEAD_DIM + k)),
        ],
        out_specs=pl.BlockSpec((NP, q_blk), lambda b, k: (b, k)),
        out_shape=jax.ShapeDtypeStruct((M_ROWS, ATT_WIDTH), BF16),
        scratch_shapes=[pltpu.VMEM((BLOCK, 4 * BLOCK), F32)],
        compiler_params=pltpu.CompilerParams(
            dimension_semantics=("arbitrary", "arbitrary"),
            vmem_limit_bytes=VMEM_LIMIT),
        name="attention",
    )(sink, proj, proj, proj)


def _ret_kernel(dec_ref, q_ref, k_ref, v_ref, g_ref, gain_ref, o_ref, st_ref, sf_ref, sb_ref):
    hd = pl.program_id(1)
    C = BLOCK
    lgf = -jnp.exp(jnp.full((1, RET_HEAD_DIM), dec_ref[0, hd], F32))
    lgb = -jnp.exp(jnp.full((1, RET_HEAD_DIM), dec_ref[1, hd], F32))
    idx = lax.broadcasted_iota(jnp.int32, (C, RET_HEAD_DIM), 0).astype(F32)
    xi_f = jnp.exp(lgf * (idx + 1.0))
    zeta_f = jnp.exp(lgf * (C - 1.0 - idx))
    xi_b = jnp.exp(lgb * (C - idx))
    zeta_b = jnp.exp(lgb * idx)
    dec_f = jnp.exp(lgf * float(C))
    dec_b = jnp.exp(lgb * float(C))
    r = lax.broadcasted_iota(jnp.int32, (C, C), 0)
    c = lax.broadcasted_iota(jnp.int32, (C, C), 1)
    diff = (r - c).astype(F32)
    dmask = jnp.where(diff >= 0.0,
                      jnp.exp(lgf[:, :C] * jnp.maximum(diff, 0.0)),
                      jnp.exp(lgb[:, :C] * jnp.maximum(-diff, 0.0)))

    def rows(ci):
        return pl.ds(pl.multiple_of(ci * C, C), C)

    def kv_outer(ci, zeta):
        kk = (k_ref[rows(ci), :].astype(F32) * zeta).astype(BF16)
        return lax.dot_general(kk, v_ref[rows(ci), :], (((0,), (0,)), ((), ())),
                               preferred_element_type=F32)

    sf_ref[...] = jnp.zeros((RET_HEAD_DIM, RET_HEAD_DIM), F32)
    sb_ref[...] = jnp.zeros((RET_HEAD_DIM, RET_HEAD_DIM), F32)

    def scan(t, carry):
        cb = NB - 1 - t
        st_ref[t, 0:RET_HEAD_DIM, :] = sf_ref[...].astype(BF16)
        st_ref[cb, RET_HEAD_DIM:, :] = sb_ref[...].astype(BF16)
        sf_ref[...] = sf_ref[...] * dec_f + kv_outer(t, zeta_f)
        sb_ref[...] = sb_ref[...] * dec_b + kv_outer(cb, zeta_b)
        return carry

    lax.fori_loop(0, NB, scan, 0)

    gain = gain_ref[...]

    def out(ci, carry):
        q = q_ref[rows(ci), :]
        k = k_ref[rows(ci), :]
        v = v_ref[rows(ci), :]
        a = lax.dot_general(q, k, (((1,), (1,)), ((), ())), preferred_element_type=F32)
        p = (a * dmask).astype(BF16)
        qf = q.astype(F32)
        qc = jnp.concatenate([(qf * xi_f).astype(BF16), (qf * xi_b).astype(BF16)], axis=1)
        y = (jnp.dot(p, v, preferred_element_type=F32)
             + jnp.dot(qc, st_ref[ci], preferred_element_type=F32))
        y = _rms(y, gain)
        gate = g_ref[rows(ci), :].astype(F32)
        gate = gate * (1.0 / (1.0 + jnp.exp(-gate)))
        o_ref[rows(ci), :] = (gate * y).astype(BF16)
        return carry

    lax.fori_loop(0, NB, out, 0)


def _retention(proj, dec, gain):
    def col_spec(col0):
        return pl.BlockSpec((NP, RET_HEAD_DIM), lambda b, h: (b, col0 // RET_HEAD_DIM + h))

    return pl.pallas_call(
        _ret_kernel,
        grid=(BATCH, RET_HEADS),
        in_specs=[
            pl.BlockSpec(memory_space=pltpu.SMEM),
            col_spec(COL_RQ), col_spec(COL_RK), col_spec(COL_RV), col_spec(COL_RG),
            pl.BlockSpec((1, RET_HEAD_DIM), lambda b, h: (0, h)),
        ],
        out_specs=pl.BlockSpec((NP, RET_HEAD_DIM), lambda b, h: (b, h)),
        out_shape=jax.ShapeDtypeStruct((M_ROWS, RET_WIDTH), BF16),
        scratch_shapes=[
            pltpu.VMEM((NB, 2 * RET_HEAD_DIM, RET_HEAD_DIM), BF16),
            pltpu.VMEM((RET_HEAD_DIM, RET_HEAD_DIM), F32),
            pltpu.VMEM((RET_HEAD_DIM, RET_HEAD_DIM), F32),
        ],
        compiler_params=pltpu.CompilerParams(
            dimension_semantics=("arbitrary", "arbitrary"),
            vmem_limit_bytes=VMEM_LIMIT),
        name="retention",
    )(dec, proj, proj, proj, proj, gain)


def _outproj_kernel(att_ref, ret_ref, w_ref, gpost_ref, gffn_ref, h_ref, ho_ref, u_ref):
    y = (jnp.dot(att_ref[...], w_ref[0:ATT_WIDTH, :], preferred_element_type=F32)
         + jnp.dot(ret_ref[...], w_ref[ATT_WIDTH:, :], preferred_element_type=F32))
    hn = h_ref[...] + _rms(y, gpost_ref[...])
    ho_ref[...] = hn
    u_ref[...] = _rms(hn, gffn_ref[...]).astype(BF16)


def _outproj(att, ret, w, gpost, gffn, h):
    row = lambda i: (i, 0)
    const = lambda i: (0, 0)
    return pl.pallas_call(
        _outproj_kernel,
        grid=(M_ROWS // TM_OUT,),
        in_specs=[
            pl.BlockSpec((TM_OUT, ATT_WIDTH), row),
            pl.BlockSpec((TM_OUT, RET_WIDTH), row),
            pl.BlockSpec((D_MODEL, D_MODEL), const),
            pl.BlockSpec((1, D_MODEL), const),
            pl.BlockSpec((1, D_MODEL), const),
            pl.BlockSpec((TM_OUT, D_MODEL), row),
        ],
        out_specs=[
            pl.BlockSpec((TM_OUT, D_MODEL), row),
            pl.BlockSpec((TM_OUT, D_MODEL), row),
        ],
        out_shape=[
            jax.ShapeDtypeStruct((M_ROWS, D_MODEL), F32),
            jax.ShapeDtypeStruct((M_ROWS, D_MODEL), BF16),
        ],
        input_output_aliases={5: 0},
        compiler_params=pltpu.CompilerParams(
            dimension_semantics=("arbitrary",),
            vmem_limit_bytes=VMEM_LIMIT),
        name="outproj",
    )(att, ret, w, gpost, gffn, h)


def _ffn_kernel(u_ref, wg_ref, wu_ref, wd_ref, gpost_ref, gnext_ref, h_ref, ho_ref, un_ref):
    i = pl.program_id(0)
    j = pl.program_id(1)
    u = u_ref[...]
    g = jnp.dot(u, wg_ref[...], preferred_element_type=F32)
    up = jnp.dot(u, wu_ref[...], preferred_element_type=F32)
    f = (g * (1.0 / (1.0 + jnp.exp(-g))) * up).astype(BF16)
    part = jnp.dot(f, wd_ref[...], preferred_element_type=F32)

    @pl.when(j == 0)
    def _():
        ho_ref[...] = part

    @pl.when(j > 0)
    def _():
        ho_ref[...] += part

    @pl.when(j == pl.num_programs(1) - 1)
    def _():
        hn = h_ref[...] + _rms(ho_ref[...], gpost_ref[...])
        ho_ref[...] = hn
        un = _rms(hn, gnext_ref[...])
        un_ref[...] = jnp.where(_valid_rows(i * TM_FFN, TM_FFN), un, 0.0).astype(BF16)


def _ffn(u, wg, wu, wd, gpost, gnext, h):
    row = lambda i, j: (i, 0)
    const = lambda i, j: (0, 0)
    return pl.pallas_call(
        _ffn_kernel,
        grid=(M_ROWS // TM_FFN, D_FF // TF_FFN),
        in_specs=[
            pl.BlockSpec((TM_FFN, D_MODEL), row),
            pl.BlockSpec((D_MODEL, TF_FFN), lambda i, j: (0, j)),
            pl.BlockSpec((D_MODEL, TF_FFN), lambda i, j: (0, j)),
            pl.BlockSpec((TF_FFN, D_MODEL), lambda i, j: (j, 0)),
            pl.BlockSpec((1, D_MODEL), const),
            pl.BlockSpec((1, D_MODEL), const),
            pl.BlockSpec((TM_FFN, D_MODEL), row),
        ],
        out_specs=[
            pl.BlockSpec((TM_FFN, D_MODEL), row),
            pl.BlockSpec((TM_FFN, D_MODEL), row),
        ],
        out_shape=[
            jax.ShapeDtypeStruct((M_ROWS, D_MODEL), F32),
            jax.ShapeDtypeStruct((M_ROWS, D_MODEL), BF16),
        ],
        input_output_aliases={6: 0},
        compiler_params=pltpu.CompilerParams(
            dimension_semantics=("arbitrary", "arbitrary"),
            vmem_limit_bytes=VMEM_LIMIT),
        name="ffn",
    )(u, wg, wu, wd, gpost, gnext, h)


def _rope_tables():
    pos = (jnp.arange(NP) - PAD_FRONT).astype(F32)
    half_a = ROT_DIM // 2
    inv_a = ROPE_THETA ** (-jnp.arange(half_a, dtype=F32) / half_a)
    ang_a = pos[:, None] * inv_a[None, :]
    cos_a, sin_a = jnp.cos(ang_a), jnp.sin(ang_a)
    rest = ATT_HEAD_DIM - ROT_DIM
    ca = jnp.concatenate([cos_a, cos_a, jnp.ones((NP, rest), F32)], axis=1)
    s1 = jnp.concatenate([jnp.zeros((NP, half_a), F32), sin_a, jnp.zeros((NP, rest), F32)], axis=1)
    s2 = jnp.concatenate([-sin_a, jnp.zeros((NP, ATT_HEAD_DIM - half_a), F32)], axis=1)
    half_r = RET_HEAD_DIM // 2
    inv_r = RET_THETA ** (-jnp.arange(half_r, dtype=F32) / half_r)
    ang_r = pos[:, None] * inv_r[None, :]
    return ca, s1, s2, jnp.cos(ang_r), jnp.sin(ang_r)


def kernel(x, meta_tokens, w_in, w_out, attn_sink, ret_decay_fwd, ret_decay_bwd, ret_norm,
           norm_mix_pre, norm_mix_post, w_gate, w_up, w_down, norm_ffn_pre, norm_ffn_post):
    tabs = _rope_tables()
    row = lambda a, l: a[l].reshape(1, -1).astype(F32)
    h, u = _embed(x, meta_tokens, row(norm_mix_pre, 0))
    for l in range(DEPTH):
        proj = _inproj(u, w_in[l].astype(BF16), tabs)
        att = _attention(proj, attn_sink[l].astype(F32))
        dec = jnp.stack([ret_decay_fwd[l], ret_decay_bwd[l]]).astype(F32)
        ret = _retention(proj, dec, row(ret_norm, l))
        h, u = _outproj(att, ret, w_out[l].astype(BF16), row(norm_mix_post, l),
                        row(norm_ffn_pre, l), h)
        h, u = _ffn(u, w_gate[l].astype(BF16), w_up[l].astype(BF16), w_down[l].astype(BF16),
                    row(norm_ffn_post, l), row(norm_mix_pre, (l + 1) % DEPTH), h)
    return h.reshape(BATCH, NP, D_MODEL)[:, BLOCK:]
```

```python
import functools

import jax
import jax.numpy as jnp
from jax import lax
from jax.experimental import pallas as pl
from jax.experimental.pallas import tpu as pltpu

F32 = jnp.float32
BF16 = jnp.bfloat16

D_MODEL = 2048
BATCH = 4
SEQ = 4096
DEPTH = 4
N_META = 16
BLOCK = 128
WINDOW = 128
PAD_FRONT = BLOCK - N_META
NP = PAD_FRONT + N_META + SEQ
NB = NP // BLOCK
M_ROWS = BATCH * NP
ATT_HEAD_DIM = 128
ATT_WIDTH = 1024
ATT_HEADS = 8
ATT_KV_HEADS = 2
ATT_GROUP = ATT_HEADS // ATT_KV_HEADS
KV_WIDTH = ATT_KV_HEADS * ATT_HEAD_DIM
ROT_DIM = 32
ROPE_THETA = 500000.0
RET_WIDTH = 1024
RET_HEAD_DIM = 256
RET_HEADS = 4
RET_THETA = 10000.0
D_FF = 5632
IN_COLS = ATT_WIDTH + 2 * KV_WIDTH + 4 * RET_WIDTH
EPS = 1e-6
NEG = -1e30

COL_AQ = 0
COL_AK = ATT_WIDTH
COL_AV = COL_AK + KV_WIDTH
COL_RQ = COL_AV + KV_WIDTH
COL_RK = COL_RQ + RET_WIDTH
COL_RV = COL_RK + RET_WIDTH
COL_RG = COL_RV + RET_WIDTH

VMEM_LIMIT = 56 * 1024 * 1024

TM_IN = 1408
TN_IN = 512
TM_OUT = 768
TM_FFN = 512
TF_FFN = 512
IN_CHUNK = 352
OUT_CHUNK = 128
FFN_CHUNK = 256


def _rms(x, g):
    return x * lax.rsqrt(jnp.mean(x * x, axis=-1, keepdims=True) + EPS) * g


def _valid_rows(row0, rows):
    r = row0 + lax.broadcasted_iota(jnp.int32, (rows, 1), 0)
    ok = r >= 0
    for b in range(BATCH):
        ok = ok & ~((r >= b * NP) & (r < b * NP + PAD_FRONT))
    return ok


def _embed_kernel(x_ref, meta_ref, g_ref, h_ref, u_ref):
    n = pl.program_id(1)

    @pl.when(n == 0)
    def _():
        h_ref[0:PAD_FRONT, :] = jnp.zeros((PAD_FRONT, D_MODEL), F32)
        u_ref[0:PAD_FRONT, :] = jnp.zeros((PAD_FRONT, D_MODEL), BF16)
        m = meta_ref[...]
        h_ref[PAD_FRONT:BLOCK, :] = m
        u_ref[PAD_FRONT:BLOCK, :] = _rms(m, g_ref[...]).astype(BF16)

    @pl.when(n > 0)
    def _():
        xv = x_ref[0]
        h_ref[...] = xv
        u_ref[...] = _rms(xv, g_ref[...]).astype(BF16)


def _embed(x, meta, g):
    return pl.pallas_call(
        _embed_kernel,
        grid=(BATCH, NB),
        in_specs=[
            pl.BlockSpec((1, BLOCK, D_MODEL), lambda b, n: (b, jnp.maximum(n - 1, 0), 0)),
            pl.BlockSpec((N_META, D_MODEL), lambda b, n: (0, 0)),
            pl.BlockSpec((1, D_MODEL), lambda b, n: (0, 0)),
        ],
        out_specs=[
            pl.BlockSpec((BLOCK, D_MODEL), lambda b, n: (b * NB + n, 0)),
            pl.BlockSpec((BLOCK, D_MODEL), lambda b, n: (b * NB + n, 0)),
        ],
        out_shape=[
            jax.ShapeDtypeStruct((M_ROWS, D_MODEL), F32),
            jax.ShapeDtypeStruct((M_ROWS, D_MODEL), BF16),
        ],
        compiler_params=pltpu.CompilerParams(
            dimension_semantics=("arbitrary", "arbitrary"),
            vmem_limit_bytes=VMEM_LIMIT),
        name="embed",
    )(x, meta, g)


def _inproj_kernel(layer_ref, u_ref, w_ref, ca_ref, s1_ref, s2_ref, cr_ref, sr_ref, o_ref):
    del layer_ref
    j = pl.program_id(1)

    def rope_att(x, r):
        return (x * ca_ref[r, :]
                + pltpu.roll(x, ROT_DIM // 2, 1) * s1_ref[r, :]
                + pltpu.roll(x, ATT_HEAD_DIM - ROT_DIM // 2, 1) * s2_ref[r, :])

    def epi_att(n_heads):
        def epi(acc, r):
            for hh in range(n_heads):
                sl = slice(hh * ATT_HEAD_DIM, (hh + 1) * ATT_HEAD_DIM)
                o_ref[r, sl] = rope_att(acc[:, sl], r).astype(BF16)
            rest = n_heads * ATT_HEAD_DIM
            if rest < TN_IN:
                o_ref[r, rest:] = acc[:, rest:].astype(BF16)
        return epi

    def epi_ret(scale):
        def epi(acc, r):
            c = cr_ref[r, :]
            s = sr_ref[r, :]
            for hh in range(TN_IN // RET_HEAD_DIM):
                base = hh * RET_HEAD_DIM
                x1 = acc[:, base:base + 128]
                x2 = acc[:, base + 128:base + 256]
                y1 = x1 * c - x2 * s
                y2 = x2 * c + x1 * s
                if scale != 1.0:
                    y1 = y1 * scale
                    y2 = y2 * scale
                o_ref[r, base:base + 128] = y1.astype(BF16)
                o_ref[r, base + 128:base + 256] = y2.astype(BF16)
        return epi

    def epi_plain(acc, r):
        o_ref[r, :] = acc.astype(BF16)

    def run(epi):
        for k in range(TM_IN // IN_CHUNK):
            r = slice(k * IN_CHUNK, (k + 1) * IN_CHUNK)
            epi(jnp.dot(u_ref[r, :], w_ref[...], preferred_element_type=F32), r)

    n_aq = ATT_WIDTH // TN_IN
    j_akv = n_aq
    j_rq = COL_RQ // TN_IN
    j_rk = COL_RK // TN_IN
    j_rv = COL_RV // TN_IN

    @pl.when(j < n_aq)
    def _():
        run(epi_att(TN_IN // ATT_HEAD_DIM))

    @pl.when(j == j_akv)
    def _():
        run(epi_att(ATT_KV_HEADS))

    @pl.when((j >= j_rq) & (j < j_rk))
    def _():
        run(epi_ret(1.0))

    @pl.when((j >= j_rk) & (j < j_rv))
    def _():
        run(epi_ret(RET_HEAD_DIM ** -0.5))

    @pl.when(j >= j_rv)
    def _():
        run(epi_plain)


def _inproj(u, w, layer, tabs):
    tiles_per_seq = NP // TM_IN
    tab_spec = pl.BlockSpec((TM_IN, 128), lambda i, j, l: (i % tiles_per_seq, 0))
    return pl.pallas_call(
        _inproj_kernel,
        grid_spec=pltpu.PrefetchScalarGridSpec(
            num_scalar_prefetch=1,
            grid=(M_ROWS // TM_IN, IN_COLS // TN_IN),
            in_specs=[
                pl.BlockSpec((TM_IN, D_MODEL), lambda i, j, l: (i, 0)),
                pl.BlockSpec((None, D_MODEL, TN_IN), lambda i, j, l: (l[0], 0, j)),
                tab_spec, tab_spec, tab_spec, tab_spec, tab_spec,
            ],
            out_specs=pl.BlockSpec((TM_IN, TN_IN), lambda i, j, l: (i, j)),
        ),
        out_shape=jax.ShapeDtypeStruct((M_ROWS, IN_COLS), BF16),
        compiler_params=pltpu.CompilerParams(
            dimension_semantics=("arbitrary", "arbitrary"),
            vmem_limit_bytes=VMEM_LIMIT),
        name="inproj",
    )(layer, u, w, *tabs)


def _att_mask(n):
    row = lax.broadcasted_iota(jnp.int32, (BLOCK, 4 * BLOCK), 0)
    col = lax.broadcasted_iota(jnp.int32, (BLOCK, 4 * BLOCK), 1)
    qi = n * BLOCK + row
    kj = (n - 1) * BLOCK + col
    is_band = col < 3 * BLOCK
    band = is_band & (jnp.abs(qi - kj) <= WINDOW) & (kj >= PAD_FRONT) & (kj < NP)
    mj = col - 3 * BLOCK
    meta = (~is_band) & (mj >= PAD_FRONT) & (jnp.abs(qi - mj) > WINDOW)
    return band | meta


def _attn_kernel(sink_ref, q_ref, k_ref, v_ref, o_ref, bias_ref):
    kv = pl.program_id(1)
    scale = ATT_HEAD_DIM ** -0.5
    rows = ATT_GROUP * BLOCK

    head = lax.broadcasted_iota(jnp.int32, (rows, 1), 0) // BLOCK
    sink = jnp.zeros((rows, 1), F32)
    for g in range(ATT_GROUP):
        sink = jnp.where(head == g, sink_ref[kv * ATT_GROUP + g], sink)

    k_meta = k_ref[0:BLOCK, :]
    v_meta = v_ref[0:BLOCK, :]

    bias_ref[...] = jnp.where(_att_mask(2), 0.0, NEG).astype(F32)

    def block(n, bias):
        r_own = pl.ds(pl.multiple_of(n * BLOCK, BLOCK), BLOCK)
        r_prev = pl.ds(pl.multiple_of(jnp.maximum(n - 1, 0) * BLOCK, BLOCK), BLOCK)
        r_next = pl.ds(pl.multiple_of(jnp.minimum(n + 1, NB - 1) * BLOCK, BLOCK), BLOCK)
        q = q_ref[r_own, :]
        qs = jnp.concatenate(
            [q[:, g * ATT_HEAD_DIM:(g + 1) * ATT_HEAD_DIM] for g in range(ATT_GROUP)], axis=0)
        ks = jnp.concatenate([k_ref[r_prev, :], k_ref[r_own, :], k_ref[r_next, :], k_meta], axis=0)
        vs = jnp.concatenate([v_ref[r_prev, :], v_ref[r_own, :], v_ref[r_next, :], v_meta], axis=0)
        s = lax.dot_general(qs, ks, (((1,), (1,)), ((), ())), preferred_element_type=F32)
        s = s * scale + jnp.concatenate([bias] * ATT_GROUP, axis=0)
        m = jnp.maximum(jnp.max(s, axis=-1, keepdims=True), sink)
        p = jnp.exp(s - m)
        den = jnp.sum(p, axis=-1, keepdims=True) + jnp.exp(sink - m)
        o = jnp.dot(p.astype(BF16), vs, preferred_element_type=F32) / den
        for g in range(ATT_GROUP):
            o_ref[r_own, g * ATT_HEAD_DIM:(g + 1) * ATT_HEAD_DIM] = (
                o[g * BLOCK:(g + 1) * BLOCK, :].astype(BF16))

    def edge(n):
        block(n, jnp.where(_att_mask(n), 0.0, NEG).astype(F32))

    edge(0)
    edge(1)

    def body(n, carry):
        block(n, bias_ref[...])
        return carry

    lax.fori_loop(2, NB - 1, body, 0, unroll=2)
    edge(NB - 1)


def _attention(proj, sink):
    q_blk = ATT_GROUP * ATT_HEAD_DIM
    return pl.pallas_call(
        _attn_kernel,
        grid=(BATCH, ATT_KV_HEADS),
        in_specs=[
            pl.BlockSpec(memory_space=pltpu.SMEM),
            pl.BlockSpec((NP, q_blk), lambda b, k: (b, k)),
            pl.BlockSpec((NP, ATT_HEAD_DIM), lambda b, k: (b, COL_AK // ATT_HEAD_DIM + k)),
            pl.BlockSpec((NP, ATT_HEAD_DIM), lambda b, k: (b, COL_AV // ATT_HEAD_DIM + k)),
        ],
        out_specs=pl.BlockSpec((NP, q_blk), lambda b, k: (b, k)),
        out_shape=jax.ShapeDtypeStruct((M_ROWS, ATT_WIDTH), BF16),
        scratch_shapes=[pltpu.VMEM((BLOCK, 4 * BLOCK), F32)],
        compiler_params=pltpu.CompilerParams(
            dimension_semantics=("arbitrary", "arbitrary"),
            vmem_limit_bytes=VMEM_LIMIT),
        name="attention",
    )(sink, proj, proj, proj)


def _ret_kernel(dec_ref, q_ref, k_ref, v_ref, g_ref, gain_ref, o_ref, st_ref, sf_ref, sb_ref):
    hd = pl.program_id(1)
    C = BLOCK
    lgf = -jnp.exp(jnp.full((1, RET_HEAD_DIM), dec_ref[0, hd], F32))
    lgb = -jnp.exp(jnp.full((1, RET_HEAD_DIM), dec_ref[1, hd], F32))
    idx = lax.broadcasted_iota(jnp.int32, (C, RET_HEAD_DIM), 0).astype(F32)
    xi_f = jnp.exp(lgf * (idx + 1.0))
    zeta_f = jnp.exp(lgf * (C - 1.0 - idx))
    xi_b = jnp.exp(lgb * (C - idx))
    zeta_b = jnp.exp(lgb * idx)
    dec_f = jnp.exp(lgf * float(C))
    dec_b = jnp.exp(lgb * float(C))
    r = lax.broadcasted_iota(jnp.int32, (C, C), 0)
    c = lax.broadcasted_iota(jnp.int32, (C, C), 1)
    diff = (r - c).astype(F32)
    dmask = jnp.where(diff >= 0.0,
                      jnp.exp(lgf[:, :C] * jnp.maximum(diff, 0.0)),
                      jnp.exp(lgb[:, :C] * jnp.maximum(-diff, 0.0)))

    def rows(ci):
        return pl.ds(pl.multiple_of(ci * C, C), C)

    def kv_outer(ci, zeta):
        kk = (k_ref[rows(ci), :].astype(F32) * zeta).astype(BF16)
        return lax.dot_general(kk, v_ref[rows(ci), :], (((0,), (0,)), ((), ())),
                               preferred_element_type=F32)

    sf_ref[...] = jnp.zeros((RET_HEAD_DIM, RET_HEAD_DIM), F32)
    sb_ref[...] = jnp.zeros((RET_HEAD_DIM, RET_HEAD_DIM), F32)

    def scan(t, carry):
        cb = NB - 1 - t
        st_ref[t, 0:RET_HEAD_DIM, :] = sf_ref[...].astype(BF16)
        st_ref[cb, RET_HEAD_DIM:, :] = sb_ref[...].astype(BF16)
        sf_ref[...] = sf_ref[...] * dec_f + kv_outer(t, zeta_f)
        sb_ref[...] = sb_ref[...] * dec_b + kv_outer(cb, zeta_b)
        return carry

    lax.fori_loop(0, NB, scan, 0, unroll=3)

    gain = gain_ref[...]

    def out(ci, carry):
        q = q_ref[rows(ci), :]
        k = k_ref[rows(ci), :]
        v = v_ref[rows(ci), :]
        a = lax.dot_general(q, k, (((1,), (1,)), ((), ())), preferred_element_type=F32)
        p = (a * dmask).astype(BF16)
        qf = q.astype(F32)
        qc = jnp.concatenate([(qf * xi_f).astype(BF16), (qf * xi_b).astype(BF16)], axis=1)
        y = (jnp.dot(p, v, preferred_element_type=F32)
             + jnp.dot(qc, st_ref[ci], preferred_element_type=F32))
        y = _rms(y, gain)
        gate = g_ref[rows(ci), :].astype(F32)
        gate = gate * (1.0 / (1.0 + jnp.exp(-gate)))
        o_ref[rows(ci), :] = (gate * y).astype(BF16)
        return carry

    lax.fori_loop(0, NB, out, 0, unroll=3)


def _retention(proj, dec, gain):
    def col_spec(col0):
        return pl.BlockSpec((NP, RET_HEAD_DIM), lambda b, h: (b, col0 // RET_HEAD_DIM + h))

    return pl.pallas_call(
        _ret_kernel,
        grid=(BATCH, RET_HEADS),
        in_specs=[
            pl.BlockSpec(memory_space=pltpu.SMEM),
            col_spec(COL_RQ), col_spec(COL_RK), col_spec(COL_RV), col_spec(COL_RG),
            pl.BlockSpec((1, RET_HEAD_DIM), lambda b, h: (0, h)),
        ],
        out_specs=pl.BlockSpec((NP, RET_HEAD_DIM), lambda b, h: (b, h)),
        out_shape=jax.ShapeDtypeStruct((M_ROWS, RET_WIDTH), BF16),
        scratch_shapes=[
            pltpu.VMEM((NB, 2 * RET_HEAD_DIM, RET_HEAD_DIM), BF16),
            pltpu.VMEM((RET_HEAD_DIM, RET_HEAD_DIM), F32),
            pltpu.VMEM((RET_HEAD_DIM, RET_HEAD_DIM), F32),
        ],
        compiler_params=pltpu.CompilerParams(
            dimension_semantics=("arbitrary", "arbitrary"),
            vmem_limit_bytes=VMEM_LIMIT),
        name="retention",
    )(dec, proj, proj, proj, proj, gain)


def _post_norms(y, h_rows, gpost, gnext, valid):
    hn = h_rows + _rms(y, gpost)
    un = _rms(hn, gnext)
    if valid is not None:
        un = jnp.where(valid, un, 0.0)
    return hn, un.astype(BF16)


def _outproj_kernel(att_ref, ret_ref, w_ref, gpost_ref, gffn_ref, h_ref, ho_ref, u_ref):
    for k in range(TM_OUT // OUT_CHUNK):
        r = slice(k * OUT_CHUNK, (k + 1) * OUT_CHUNK)
        y = (jnp.dot(att_ref[r, :], w_ref[0:ATT_WIDTH, :], preferred_element_type=F32)
             + jnp.dot(ret_ref[r, :], w_ref[ATT_WIDTH:, :], preferred_element_type=F32))
        hn, un = _post_norms(y, h_ref[r, :], gpost_ref[...], gffn_ref[...], None)
        ho_ref[r, :] = hn
        u_ref[r, :] = un


def _outproj(att, ret, w, layer, gpost, gffn, h):
    row = lambda i: (i, 0)
    const = lambda i: (0, 0)
    return pl.pallas_call(
        _outproj_kernel,
        grid=(M_ROWS // TM_OUT,),
        in_specs=[
            pl.BlockSpec((TM_OUT, ATT_WIDTH), row),
            pl.BlockSpec((TM_OUT, RET_WIDTH), row),
            pl.BlockSpec((None, D_MODEL, D_MODEL), lambda i: (layer, 0, 0)),
            pl.BlockSpec((1, D_MODEL), const),
            pl.BlockSpec((1, D_MODEL), const),
            pl.BlockSpec((TM_OUT, D_MODEL), row),
        ],
        out_specs=[
            pl.BlockSpec((TM_OUT, D_MODEL), row),
            pl.BlockSpec((TM_OUT, D_MODEL), row),
        ],
        out_shape=[
            jax.ShapeDtypeStruct((M_ROWS, D_MODEL), F32),
            jax.ShapeDtypeStruct((M_ROWS, D_MODEL), BF16),
        ],
        input_output_aliases={5: 0},
        compiler_params=pltpu.CompilerParams(
            dimension_semantics=("arbitrary",),
            vmem_limit_bytes=VMEM_LIMIT),
        name="outproj",
    )(att, ret, w, gpost, gffn, h)


def _ffn_kernel(layer_ref, u_ref, wg_ref, wu_ref, wd_ref, gpost_ref, gnext_ref, h_ref, ho_ref,
                *maybe_un_ref):
    del layer_ref
    i = pl.program_id(0)
    j = pl.program_id(1)
    last = pl.num_programs(1) - 1

    def swiglu(rows):
        u = u_ref[rows, :]
        g = jnp.dot(u, wg_ref[...], preferred_element_type=F32)
        up = jnp.dot(u, wu_ref[...], preferred_element_type=F32)
        f = (g * (1.0 / (1.0 + jnp.exp(-g))) * up).astype(BF16)
        return jnp.dot(f, wd_ref[...], preferred_element_type=F32)

    @pl.when(j == 0)
    def _():
        ho_ref[...] = jnp.zeros((TM_FFN, D_MODEL), F32)

    @pl.when(j < last)
    def _():
        ho_ref[...] += swiglu(slice(None))

    @pl.when(j == last)
    def _():
        for k in range(TM_FFN // FFN_CHUNK):
            r = slice(k * FFN_CHUNK, (k + 1) * FFN_CHUNK)
            y = ho_ref[r, :] + swiglu(r)
            if maybe_un_ref:
                valid = _valid_rows(i * TM_FFN + k * FFN_CHUNK, FFN_CHUNK)
                hn, un = _post_norms(y, h_ref[r, :], gpost_ref[...], gnext_ref[...], valid)
                maybe_un_ref[0][r, :] = un
            else:
                hn = h_ref[r, :] + _rms(y, gpost_ref[...])
            ho_ref[r, :] = hn


def _ffn(u, wg, wu, wd, layer, gpost, gnext, h, emit_next):
    row = lambda i, j, l: (i, 0)
    const = lambda i, j, l: (0, 0)
    out_specs = [pl.BlockSpec((TM_FFN, D_MODEL), row)]
    out_shape = [jax.ShapeDtypeStruct((M_ROWS, D_MODEL), F32)]
    if emit_next:
        out_specs.append(pl.BlockSpec((TM_FFN, D_MODEL), row))
        out_shape.append(jax.ShapeDtypeStruct((M_ROWS, D_MODEL), BF16))
    return pl.pallas_call(
        _ffn_kernel,
        grid_spec=pltpu.PrefetchScalarGridSpec(
            num_scalar_prefetch=1,
            grid=(M_ROWS // TM_FFN, D_FF // TF_FFN),
            in_specs=[
                pl.BlockSpec((TM_FFN, D_MODEL), row),
                pl.BlockSpec((None, D_MODEL, TF_FFN), lambda i, j, l: (l[0], 0, j)),
                pl.BlockSpec((None, D_MODEL, TF_FFN), lambda i, j, l: (l[0], 0, j)),
                pl.BlockSpec((None, TF_FFN, D_MODEL), lambda i, j, l: (l[0], j, 0)),
                pl.BlockSpec((1, D_MODEL), const),
                pl.BlockSpec((1, D_MODEL), const),
                pl.BlockSpec((TM_FFN, D_MODEL), row),
            ],
            out_specs=out_specs,
        ),
        out_shape=out_shape,
        input_output_aliases={7: 0},
        compiler_params=pltpu.CompilerParams(
            dimension_semantics=("arbitrary", "arbitrary"),
            vmem_limit_bytes=VMEM_LIMIT),
        name="ffn",
    )(layer, u, wg, wu, wd, gpost, gnext, h)


def _rope_tables():
    pos = (jnp.arange(NP) - PAD_FRONT).astype(F32)
    half_a = ROT_DIM // 2
    inv_a = ROPE_THETA ** (-jnp.arange(half_a, dtype=F32) / half_a)
    ang_a = pos[:, None] * inv_a[None, :]
    cos_a, sin_a = jnp.cos(ang_a), jnp.sin(ang_a)
    rest = ATT_HEAD_DIM - ROT_DIM
    ca = jnp.concatenate([cos_a, cos_a, jnp.ones((NP, rest), F32)], axis=1)
    s1 = jnp.concatenate([jnp.zeros((NP, half_a), F32), sin_a, jnp.zeros((NP, rest), F32)], axis=1)
    s2 = jnp.concatenate([-sin_a, jnp.zeros((NP, ATT_HEAD_DIM - half_a), F32)], axis=1)
    half_r = RET_HEAD_DIM // 2
    inv_r = RET_THETA ** (-jnp.arange(half_r, dtype=F32) / half_r)
    ang_r = pos[:, None] * inv_r[None, :]
    return ca, s1, s2, jnp.cos(ang_r), jnp.sin(ang_r)


def kernel(x, meta_tokens, w_in, w_out, attn_sink, ret_decay_fwd, ret_decay_bwd, ret_norm,
           norm_mix_pre, norm_mix_post, w_gate, w_up, w_down, norm_ffn_pre, norm_ffn_post):
    tabs = _rope_tables()
    row = lambda a, l: a[l].reshape(1, -1).astype(F32)
    w_in, w_out, w_gate, w_up, w_down = (
        w.astype(BF16) for w in (w_in, w_out, w_gate, w_up, w_down))
    h, u = _embed(x, meta_tokens, row(norm_mix_pre, 0))
    for l in range(DEPTH):
        layer = jnp.full((1,), l, jnp.int32)
        proj = _inproj(u, w_in, layer, tabs)
        att = _attention(proj, attn_sink[l].astype(F32))
        dec = jnp.stack([ret_decay_fwd[l], ret_decay_bwd[l]]).astype(F32)
        ret = _retention(proj, dec, row(ret_norm, l))
        h, u = _outproj(att, ret, w_out, l, row(norm_mix_post, l), row(norm_ffn_pre, l), h)
        emit_next = l + 1 < DEPTH
        gnext = row(norm_mix_pre, l + 1) if emit_next else row(norm_mix_pre, l)
        res = _ffn(u, w_gate, w_up, w_down, layer, row(norm_ffn_post, l), gnext, h, emit_next)
        h, u = res if emit_next else (res[0], None)
    return h.reshape(BATCH, NP, D_MODEL)[:, BLOCK:]
```

```python
import functools

import jax
import jax.numpy as jnp
from jax import lax
from jax.experimental import pallas as pl
from jax.experimental.pallas import tpu as pltpu

F32 = jnp.float32
BF16 = jnp.bfloat16

D_MODEL = 2048
BATCH = 4
SEQ = 4096
DEPTH = 4
N_META = 16
BLOCK = 128
WINDOW = 128
PAD_FRONT = BLOCK - N_META
NP = PAD_FRONT + N_META + SEQ
NB = NP // BLOCK
M_ROWS = BATCH * NP
ATT_HEAD_DIM = 128
ATT_WIDTH = 1024
ATT_HEADS = 8
ATT_KV_HEADS = 2
ATT_GROUP = ATT_HEADS // ATT_KV_HEADS
KV_WIDTH = ATT_KV_HEADS * ATT_HEAD_DIM
ROT_DIM = 32
ROPE_THETA = 500000.0
RET_WIDTH = 1024
RET_HEAD_DIM = 256
RET_HEADS = 4
RET_THETA = 10000.0
D_FF = 5632
IN_COLS = ATT_WIDTH + 2 * KV_WIDTH + 4 * RET_WIDTH
EPS = 1e-6
NEG = -1e30
LOG2E = 1.4426950408889634
ATT_QSCALE = ATT_HEAD_DIM ** -0.5 * LOG2E

COL_AQ = 0
COL_AK = ATT_WIDTH
COL_AV = COL_AK + KV_WIDTH
COL_RQ = COL_AV + KV_WIDTH
COL_RK = COL_RQ + RET_WIDTH
COL_RV = COL_RK + RET_WIDTH
COL_RG = COL_RV + RET_WIDTH

VMEM_LIMIT = 56 * 1024 * 1024

TM_IN = 1408
TN_IN = 512
TM_OUT = 768
TM_FFN = 768
TF_FFN = 512
EMBED_BLOCKS = 3
RET_CHUNK = 256
RET_CHUNKS = (NP - BLOCK) // RET_CHUNK
IN_CHUNK = 352
OUT_CHUNK = 128
FFN_CHUNK = 256


def _rms(x, g):
    return x * lax.rsqrt(jnp.mean(x * x, axis=-1, keepdims=True) + EPS) * g


def _valid_rows(row0, rows):
    r = row0 + lax.broadcasted_iota(jnp.int32, (rows, 1), 0)
    ok = r >= 0
    for b in range(BATCH):
        ok = ok & ~((r >= b * NP) & (r < b * NP + PAD_FRONT))
    return ok


def _embed_kernel(*refs):
    x_refs = refs[:EMBED_BLOCKS]
    meta_ref, g_ref, h_ref, u_ref = refs[EMBED_BLOCKS:]
    n = pl.program_id(1)

    def copy(x_ref, rows):
        xv = x_ref[0]
        h_ref[rows, :] = xv
        u_ref[rows, :] = _rms(xv, g_ref[...]).astype(BF16)

    @pl.when(n == 0)
    def _():
        h_ref[0:PAD_FRONT, :] = jnp.zeros((PAD_FRONT, D_MODEL), F32)
        u_ref[0:PAD_FRONT, :] = jnp.zeros((PAD_FRONT, D_MODEL), BF16)
        m = meta_ref[...]
        h_ref[PAD_FRONT:BLOCK, :] = m
        u_ref[PAD_FRONT:BLOCK, :] = _rms(m, g_ref[...]).astype(BF16)

    @pl.when(n > 0)
    def _():
        copy(x_refs[0], slice(0, BLOCK))

    for t in range(1, EMBED_BLOCKS):
        copy(x_refs[t], slice(t * BLOCK, (t + 1) * BLOCK))


def _embed(x, meta, g):
    rows = EMBED_BLOCKS * BLOCK

    def x_spec(t):
        return pl.BlockSpec(
            (1, BLOCK, D_MODEL), lambda b, n: (b, jnp.maximum(n * EMBED_BLOCKS + t - 1, 0), 0))

    return pl.pallas_call(
        _embed_kernel,
        grid=(BATCH, NB // EMBED_BLOCKS),
        in_specs=[x_spec(t) for t in range(EMBED_BLOCKS)] + [
            pl.BlockSpec((N_META, D_MODEL), lambda b, n: (0, 0)),
            pl.BlockSpec((1, D_MODEL), lambda b, n: (0, 0)),
        ],
        out_specs=[
            pl.BlockSpec((rows, D_MODEL), lambda b, n: (b * (NB // EMBED_BLOCKS) + n, 0)),
            pl.BlockSpec((rows, D_MODEL), lambda b, n: (b * (NB // EMBED_BLOCKS) + n, 0)),
        ],
        out_shape=[
            jax.ShapeDtypeStruct((M_ROWS, D_MODEL), F32),
            jax.ShapeDtypeStruct((M_ROWS, D_MODEL), BF16),
        ],
        compiler_params=pltpu.CompilerParams(
            dimension_semantics=("arbitrary", "arbitrary"),
            vmem_limit_bytes=VMEM_LIMIT),
        name="embed",
    )(*([x] * EMBED_BLOCKS), meta, g)


def _inproj_kernel(layer_ref, u_ref, w_ref, ca_ref, s1_ref, s2_ref, cr_ref, sr_ref, o_ref):
    del layer_ref
    j = pl.program_id(1)

    def rope_att(x, r):
        return (x * ca_ref[r, :]
                + pltpu.roll(x, ROT_DIM // 2, 1) * s1_ref[r, :]
                + pltpu.roll(x, ATT_HEAD_DIM - ROT_DIM // 2, 1) * s2_ref[r, :])

    def epi_att(n_heads, scale):
        def epi(acc, r):
            for hh in range(n_heads):
                sl = slice(hh * ATT_HEAD_DIM, (hh + 1) * ATT_HEAD_DIM)
                y = rope_att(acc[:, sl], r)
                if scale != 1.0:
                    y = y * scale
                o_ref[r, sl] = y.astype(BF16)
            rest = n_heads * ATT_HEAD_DIM
            if rest < TN_IN:
                o_ref[r, rest:] = acc[:, rest:].astype(BF16)
        return epi

    def epi_ret(scale):
        def epi(acc, r):
            c = cr_ref[r, :]
            s = sr_ref[r, :]
            for hh in range(TN_IN // RET_HEAD_DIM):
                base = hh * RET_HEAD_DIM
                x1 = acc[:, base:base + 128]
                x2 = acc[:, base + 128:base + 256]
                y1 = x1 * c - x2 * s
                y2 = x2 * c + x1 * s
                if scale != 1.0:
                    y1 = y1 * scale
                    y2 = y2 * scale
                o_ref[r, base:base + 128] = y1.astype(BF16)
                o_ref[r, base + 128:base + 256] = y2.astype(BF16)
        return epi

    def epi_plain(acc, r):
        o_ref[r, :] = acc.astype(BF16)

    def run(epi):
        for k in range(TM_IN // IN_CHUNK):
            r = slice(k * IN_CHUNK, (k + 1) * IN_CHUNK)
            epi(jnp.dot(u_ref[r, :], w_ref[...], preferred_element_type=F32), r)

    n_aq = ATT_WIDTH // TN_IN
    j_akv = n_aq
    j_rq = COL_RQ // TN_IN
    j_rk = COL_RK // TN_IN
    j_rv = COL_RV // TN_IN

    @pl.when(j < n_aq)
    def _():
        run(epi_att(TN_IN // ATT_HEAD_DIM, ATT_QSCALE))

    @pl.when(j == j_akv)
    def _():
        run(epi_att(ATT_KV_HEADS, 1.0))

    @pl.when((j >= j_rq) & (j < j_rk))
    def _():
        run(epi_ret(1.0))

    @pl.when((j >= j_rk) & (j < j_rv))
    def _():
        run(epi_ret(RET_HEAD_DIM ** -0.5))

    @pl.when(j >= j_rv)
    def _():
        run(epi_plain)


def _inproj(u, w, layer, tabs):
    tiles_per_seq = NP // TM_IN
    tab_spec = pl.BlockSpec((TM_IN, 128), lambda i, j, l: (i % tiles_per_seq, 0))
    return pl.pallas_call(
        _inproj_kernel,
        grid_spec=pltpu.PrefetchScalarGridSpec(
            num_scalar_prefetch=1,
            grid=(M_ROWS // TM_IN, IN_COLS // TN_IN),
            in_specs=[
                pl.BlockSpec((TM_IN, D_MODEL), lambda i, j, l: (i, 0)),
                pl.BlockSpec((None, D_MODEL, TN_IN), lambda i, j, l: (l[0], 0, j)),
                tab_spec, tab_spec, tab_spec, tab_spec, tab_spec,
            ],
            out_specs=pl.BlockSpec((TM_IN, TN_IN), lambda i, j, l: (i, j)),
        ),
        out_shape=jax.ShapeDtypeStruct((M_ROWS, IN_COLS), BF16),
        compiler_params=pltpu.CompilerParams(
            dimension_semantics=("arbitrary", "arbitrary"),
            vmem_limit_bytes=VMEM_LIMIT),
        name="inproj",
    )(layer, u, w, *tabs)


def _att_mask(n):
    row = lax.broadcasted_iota(jnp.int32, (BLOCK, 4 * BLOCK), 0)
    col = lax.broadcasted_iota(jnp.int32, (BLOCK, 4 * BLOCK), 1)
    qi = n * BLOCK + row
    kj = (n - 1) * BLOCK + col
    is_band = col < 3 * BLOCK
    band = is_band & (jnp.abs(qi - kj) <= WINDOW) & (kj >= PAD_FRONT) & (kj < NP)
    mj = col - 3 * BLOCK
    meta = (~is_band) & (mj >= PAD_FRONT) & (jnp.abs(qi - mj) > WINDOW)
    return band | meta


def _attn_kernel(sink_ref, q_ref, k_ref, v_ref, o_ref, bias_ref):
    kv = pl.program_id(1)
    rows = ATT_GROUP * BLOCK

    head = lax.broadcasted_iota(jnp.int32, (rows, 1), 0) // BLOCK
    sink = jnp.zeros((rows, 1), F32)
    for g in range(ATT_GROUP):
        sink = jnp.where(head == g, sink_ref[kv * ATT_GROUP + g] * LOG2E, sink)

    k_meta = k_ref[0:BLOCK, :]
    v_meta = v_ref[0:BLOCK, :]
    ones = jnp.ones((4 * BLOCK, ATT_HEAD_DIM), BF16)

    bias_ref[...] = jnp.where(_att_mask(2), 0.0, NEG).astype(F32)

    def block(n, bias, own_bias):
        r_own = pl.ds(pl.multiple_of(n * BLOCK, BLOCK), BLOCK)
        r_prev = pl.ds(pl.multiple_of(jnp.maximum(n - 1, 0) * BLOCK, BLOCK), BLOCK)
        r_next = pl.ds(pl.multiple_of(jnp.minimum(n + 1, NB - 1) * BLOCK, BLOCK), BLOCK)
        q = q_ref[r_own, :]
        qs = jnp.concatenate(
            [q[:, g * ATT_HEAD_DIM:(g + 1) * ATT_HEAD_DIM] for g in range(ATT_GROUP)], axis=0)
        ks = jnp.concatenate([k_ref[r_prev, :], k_ref[r_own, :], k_ref[r_next, :], k_meta], axis=0)
        vs = jnp.concatenate([v_ref[r_prev, :], v_ref[r_own, :], v_ref[r_next, :], v_meta], axis=0)
        vs = jnp.concatenate([vs, ones], axis=1)
        s = lax.dot_general(qs, ks, (((1,), (1,)), ((), ())), preferred_element_type=F32)
        bias4 = jnp.concatenate([bias] * ATT_GROUP, axis=0)
        if own_bias:
            s = s + bias4
        else:
            s = jnp.concatenate([s[:, :BLOCK] + bias4[:, :BLOCK], s[:, BLOCK:2 * BLOCK],
                                 s[:, 2 * BLOCK:] + bias4[:, 2 * BLOCK:]], axis=1)
        m = jnp.maximum(jnp.max(s, axis=-1, keepdims=True), sink)
        p = jnp.exp2(s - m).astype(BF16)
        o = jnp.dot(p, vs, preferred_element_type=F32)
        o = o[:, :ATT_HEAD_DIM] / (o[:, ATT_HEAD_DIM:] + jnp.exp2(sink - m))
        for g in range(ATT_GROUP):
            o_ref[r_own, g * ATT_HEAD_DIM:(g + 1) * ATT_HEAD_DIM] = (
                o[g * BLOCK:(g + 1) * BLOCK, :].astype(BF16))

    def edge(n):
        block(n, jnp.where(_att_mask(n), 0.0, NEG).astype(F32), True)

    edge(0)
    edge(1)

    def body(n, carry):
        block(n, bias_ref[...], False)
        return carry

    lax.fori_loop(2, NB - 1, body, 0, unroll=10)
    edge(NB - 1)


def _attention(proj, sink):
    q_blk = ATT_GROUP * ATT_HEAD_DIM
    return pl.pallas_call(
        _attn_kernel,
        grid=(BATCH, ATT_KV_HEADS),
        in_specs=[
            pl.BlockSpec(memory_space=pltpu.SMEM),
            pl.BlockSpec((NP, q_blk), lambda b, k: (b, k)),
            pl.BlockSpec((NP, ATT_HEAD_DIM), lambda b, k: (b, COL_AK // ATT_HEAD_DIM + k)),
            pl.BlockSpec((NP, ATT_HEAD_DIM), lambda b, k: (b, COL_AV // ATT_HEAD_DIM + k)),
        ],
        out_specs=pl.BlockSpec((NP, q_blk), lambda b, k: (b, k)),
        out_shape=jax.ShapeDtypeStruct((M_ROWS, ATT_WIDTH), BF16),
        scratch_shapes=[pltpu.VMEM((BLOCK, 4 * BLOCK), F32)],
        compiler_params=pltpu.CompilerParams(
            dimension_semantics=("arbitrary", "arbitrary"),
            vmem_limit_bytes=VMEM_LIMIT),
        name="attention",
    )(sink, proj, proj, proj)


def _ret_kernel(dec_ref, q_ref, k_ref, v_ref, g_ref, gain_ref, o_ref, st_ref, sf_ref, sb_ref):
    hd = pl.program_id(1)
    lgf = -jnp.exp(jnp.full((1, RET_HEAD_DIM), dec_ref[0, hd], F32))
    lgb = -jnp.exp(jnp.full((1, RET_HEAD_DIM), dec_ref[1, hd], F32))

    def weights(L):
        idx = lax.broadcasted_iota(jnp.int32, (L, RET_HEAD_DIM), 0).astype(F32)
        r = lax.broadcasted_iota(jnp.int32, (L, L), 0)
        c = lax.broadcasted_iota(jnp.int32, (L, L), 1)
        diff = (r - c).astype(F32)
        return dict(
            xi_f=jnp.exp(lgf * (idx + 1.0)).astype(BF16),
            zeta_f=jnp.exp(lgf * (L - 1.0 - idx)).astype(BF16),
            xi_b=jnp.exp(lgb * (L - idx)).astype(BF16),
            zeta_b=jnp.exp(lgb * idx).astype(BF16),
            dec_f=jnp.exp(lgf * float(L)),
            dec_b=jnp.exp(lgb * float(L)),
            dmask=jnp.where(diff >= 0.0,
                            jnp.exp(lgf[:, :L] * jnp.maximum(diff, 0.0)),
                            jnp.exp(lgb[:, :L] * jnp.maximum(-diff, 0.0))))

    w_head = weights(BLOCK)
    w_main = weights(RET_CHUNK)

    def main_rows(c):
        return pl.ds(pl.multiple_of(BLOCK + (c - 1) * RET_CHUNK, BLOCK), RET_CHUNK)

    head_rows = slice(0, BLOCK)

    def kv_outer(rows, zeta):
        return lax.dot_general(k_ref[rows, :] * zeta, v_ref[rows, :], (((0,), (0,)), ((), ())),
                               preferred_element_type=F32)

    zeros = jnp.zeros((RET_HEAD_DIM, RET_HEAD_DIM), F32)
    st_ref[0, 0:RET_HEAD_DIM, :] = zeros.astype(BF16)
    sf_ref[...] = kv_outer(head_rows, w_head["zeta_f"])
    sb_ref[...] = zeros

    def scan(t, carry):
        cf = 1 + t
        cb = RET_CHUNKS - t
        st_ref[cf, 0:RET_HEAD_DIM, :] = sf_ref[...].astype(BF16)
        st_ref[cb, RET_HEAD_DIM:, :] = sb_ref[...].astype(BF16)
        sf_ref[...] = sf_ref[...] * w_main["dec_f"] + kv_outer(main_rows(cf), w_main["zeta_f"])
        sb_ref[...] = sb_ref[...] * w_main["dec_b"] + kv_outer(main_rows(cb), w_main["zeta_b"])
        return carry

    lax.fori_loop(0, RET_CHUNKS, scan, 0, unroll=2)
    st_ref[0, RET_HEAD_DIM:, :] = sb_ref[...].astype(BF16)

    gain = gain_ref[...]

    def out(rows, c, w):
        q = q_ref[rows, :]
        k = k_ref[rows, :]
        v = v_ref[rows, :]
        a = lax.dot_general(q, k, (((1,), (1,)), ((), ())), preferred_element_type=F32)
        p = (a * w["dmask"]).astype(BF16)
        qc = jnp.concatenate([q * w["xi_f"], q * w["xi_b"]], axis=1)
        y = (jnp.dot(p, v, preferred_element_type=F32)
             + jnp.dot(qc, st_ref[c], preferred_element_type=F32))
        y = _rms(y, gain)
        gate = g_ref[rows, :].astype(F32)
        gate = gate * (1.0 / (1.0 + jnp.exp(-gate)))
        o_ref[rows, :] = (gate * y).astype(BF16)

    out(head_rows, 0, w_head)

    def out_main(t, carry):
        c = 1 + t
        out(main_rows(c), c, w_main)
        return carry

    lax.fori_loop(0, RET_CHUNKS, out_main, 0, unroll=8)


def _retention(proj, dec, gain):
    def col_spec(col0):
        return pl.BlockSpec((NP, RET_HEAD_DIM), lambda b, h: (b, col0 // RET_HEAD_DIM + h))

    return pl.pallas_call(
        _ret_kernel,
        grid=(BATCH, RET_HEADS),
        in_specs=[
            pl.BlockSpec(memory_space=pltpu.SMEM),
            col_spec(COL_RQ), col_spec(COL_RK), col_spec(COL_RV), col_spec(COL_RG),
            pl.BlockSpec((1, RET_HEAD_DIM), lambda b, h: (0, h)),
        ],
        out_specs=pl.BlockSpec((NP, RET_HEAD_DIM), lambda b, h: (b, h)),
        out_shape=jax.ShapeDtypeStruct((M_ROWS, RET_WIDTH), BF16),
        scratch_shapes=[
            pltpu.VMEM((RET_CHUNKS + 1, 2 * RET_HEAD_DIM, RET_HEAD_DIM), BF16),
            pltpu.VMEM((RET_HEAD_DIM, RET_HEAD_DIM), F32),
            pltpu.VMEM((RET_HEAD_DIM, RET_HEAD_DIM), F32),
        ],
        compiler_params=pltpu.CompilerParams(
            dimension_semantics=("arbitrary", "arbitrary"),
            vmem_limit_bytes=VMEM_LIMIT),
        name="retention",
    )(dec, proj, proj, proj, proj, gain)


def _post_norms(y, h_rows, gpost, gnext, valid):
    hn = h_rows + _rms(y, gpost)
    un = _rms(hn, gnext)
    if valid is not None:
        un = jnp.where(valid, un, 0.0)
    return hn, un.astype(BF16)


def _outproj_kernel(att_ref, ret_ref, w_ref, gpost_ref, gffn_ref, h_ref, ho_ref, u_ref):
    for k in range(TM_OUT // OUT_CHUNK):
        r = slice(k * OUT_CHUNK, (k + 1) * OUT_CHUNK)
        y = (jnp.dot(att_ref[r, :], w_ref[0:ATT_WIDTH, :], preferred_element_type=F32)
             + jnp.dot(ret_ref[r, :], w_ref[ATT_WIDTH:, :], preferred_element_type=F32))
        hn, un = _post_norms(y, h_ref[r, :], gpost_ref[...], gffn_ref[...], None)
        ho_ref[r, :] = hn
        u_ref[r, :] = un


def _outproj(att, ret, w, layer, gpost, gffn, h):
    row = lambda i: (i, 0)
    const = lambda i: (0, 0)
    return pl.pallas_call(
        _outproj_kernel,
        grid=(M_ROWS // TM_OUT,),
        in_specs=[
            pl.BlockSpec((TM_OUT, ATT_WIDTH), row),
            pl.BlockSpec((TM_OUT, RET_WIDTH), row),
            pl.BlockSpec((None, D_MODEL, D_MODEL), lambda i: (layer, 0, 0)),
            pl.BlockSpec((1, D_MODEL), const),
            pl.BlockSpec((1, D_MODEL), const),
            pl.BlockSpec((TM_OUT, D_MODEL), row),
        ],
        out_specs=[
            pl.BlockSpec((TM_OUT, D_MODEL), row),
            pl.BlockSpec((TM_OUT, D_MODEL), row),
        ],
        out_shape=[
            jax.ShapeDtypeStruct((M_ROWS, D_MODEL), F32),
            jax.ShapeDtypeStruct((M_ROWS, D_MODEL), BF16),
        ],
        input_output_aliases={5: 0},
        compiler_params=pltpu.CompilerParams(
            dimension_semantics=("arbitrary",),
            vmem_limit_bytes=VMEM_LIMIT),
        name="outproj",
    )(att, ret, w, gpost, gffn, h)


def _ffn_kernel(layer_ref, u_ref, wg_ref, wu_ref, wd_ref, gpost_ref, gnext_ref, h_ref, ho_ref,
                *maybe_un_ref):
    del layer_ref
    i = pl.program_id(0)
    j = pl.program_id(1)
    last = pl.num_programs(1) - 1

    def swiglu(rows):
        u = u_ref[rows, :]
        g = jnp.dot(u, wg_ref[...], preferred_element_type=F32)
        up = jnp.dot(u, wu_ref[...], preferred_element_type=F32)
        f = (g * (1.0 / (1.0 + jnp.exp(-g))) * up).astype(BF16)
        return jnp.dot(f, wd_ref[...], preferred_element_type=F32)

    @pl.when(j == 0)
    def _():
        ho_ref[...] = jnp.zeros((TM_FFN, D_MODEL), F32)

    @pl.when(j < last)
    def _():
        ho_ref[...] += swiglu(slice(None))

    @pl.when(j == last)
    def _():
        for k in range(TM_FFN // FFN_CHUNK):
            r = slice(k * FFN_CHUNK, (k + 1) * FFN_CHUNK)
            y = ho_ref[r, :] + swiglu(r)
            if maybe_un_ref:
                valid = _valid_rows(i * TM_FFN + k * FFN_CHUNK, FFN_CHUNK)
                hn, un = _post_norms(y, h_ref[r, :], gpost_ref[...], gnext_ref[...], valid)
                maybe_un_ref[0][r, :] = un
            else:
                hn = h_ref[r, :] + _rms(y, gpost_ref[...])
            ho_ref[r, :] = hn


def _ffn(u, wg, wu, wd, layer, gpost, gnext, h, emit_next):
    row = lambda i, j, l: (i, 0)
    const = lambda i, j, l: (0, 0)
    out_specs = [pl.BlockSpec((TM_FFN, D_MODEL), row)]
    out_shape = [jax.ShapeDtypeStruct((M_ROWS, D_MODEL), F32)]
    if emit_next:
        out_specs.append(pl.BlockSpec((TM_FFN, D_MODEL), row))
        out_shape.append(jax.ShapeDtypeStruct((M_ROWS, D_MODEL), BF16))
    return pl.pallas_call(
        _ffn_kernel,
        grid_spec=pltpu.PrefetchScalarGridSpec(
            num_scalar_prefetch=1,
            grid=(M_ROWS // TM_FFN, D_FF // TF_FFN),
            in_specs=[
                pl.BlockSpec((TM_FFN, D_MODEL), row),
                pl.BlockSpec((None, D_MODEL, TF_FFN), lambda i, j, l: (l[0], 0, j)),
                pl.BlockSpec((None, D_MODEL, TF_FFN), lambda i, j, l: (l[0], 0, j)),
                pl.BlockSpec((None, TF_FFN, D_MODEL), lambda i, j, l: (l[0], j, 0)),
                pl.BlockSpec((1, D_MODEL), const),
                pl.BlockSpec((1, D_MODEL), const),
                pl.BlockSpec((TM_FFN, D_MODEL), row),
            ],
            out_specs=out_specs,
        ),
        out_shape=out_shape,
        input_output_aliases={7: 0},
        compiler_params=pltpu.CompilerParams(
            dimension_semantics=("arbitrary", "arbitrary"),
            vmem_limit_bytes=VMEM_LIMIT),
        name="ffn",
    )(layer, u, wg, wu, wd, gpost, gnext, h)


def _rope_tables():
    pos = (jnp.arange(NP) - PAD_FRONT).astype(F32)
    half_a = ROT_DIM // 2
    inv_a = ROPE_THETA ** (-jnp.arange(half_a, dtype=F32) / half_a)
    ang_a = pos[:, None] * inv_a[None, :]
    cos_a, sin_a = jnp.cos(ang_a), jnp.sin(ang_a)
    rest = ATT_HEAD_DIM - ROT_DIM
    ca = jnp.concatenate([cos_a, cos_a, jnp.ones((NP, rest), F32)], axis=1)
    s1 = jnp.concatenate([jnp.zeros((NP, half_a), F32), sin_a, jnp.zeros((NP, rest), F32)], axis=1)
    s2 = jnp.concatenate([-sin_a, jnp.zeros((NP, ATT_HEAD_DIM - half_a), F32)], axis=1)
    half_r = RET_HEAD_DIM // 2
    inv_r = RET_THETA ** (-jnp.arange(half_r, dtype=F32) / half_r)
    ang_r = pos[:, None] * inv_r[None, :]
    return ca, s1, s2, jnp.cos(ang_r), jnp.sin(ang_r)


def kernel(x, meta_tokens, w_in, w_out, attn_sink, ret_decay_fwd, ret_decay_bwd, ret_norm,
           norm_mix_pre, norm_mix_post, w_gate, w_up, w_down, norm_ffn_pre, norm_ffn_post):
    tabs = _rope_tables()
    row = lambda a, l: a[l].reshape(1, -1).astype(F32)
    w_in, w_out, w_gate, w_up, w_down = (
        w.astype(BF16) for w in (w_in, w_out, w_gate, w_up, w_down))
    h, u = _embed(x, meta_tokens, row(norm_mix_pre, 0))
    for l in range(DEPTH):
        layer = jnp.full((1,), l, jnp.int32)
        proj = _inproj(u, w_in, layer, tabs)
        att = _attention(proj, attn_sink[l].astype(F32))
        dec = jnp.stack([ret_decay_fwd[l], ret_decay_bwd[l]]).astype(F32)
        ret = _retention(proj, dec, row(ret_norm, l))
        h, u = _outproj(att, ret, w_out, l, row(norm_mix_post, l), row(norm_ffn_pre, l), h)
        emit_next = l + 1 < DEPTH
        gnext = row(norm_mix_pre, l + 1) if emit_next else row(norm_mix_pre, l)
        res = _ffn(u, w_gate, w_up, w_down, layer, row(norm_ffn_post, l), gnext, h, emit_next)
        h, u = res if emit_next else (res[0], None)
    return h.reshape(BATCH, NP, D_MODEL)[:, BLOCK:]
```

```python
import functools

import jax
import jax.numpy as jnp
from jax import lax
from jax.experimental import pallas as pl
from jax.experimental.pallas import tpu as pltpu

F32 = jnp.float32
BF16 = jnp.bfloat16

D_MODEL = 2048
BATCH = 4
SEQ = 4096
DEPTH = 4
N_META = 16
BLOCK = 128
WINDOW = 128
PAD_FRONT = BLOCK - N_META
NP = PAD_FRONT + N_META + SEQ
NB = NP // BLOCK
M_ROWS = BATCH * NP
ATT_HEAD_DIM = 128
ATT_WIDTH = 1024
ATT_HEADS = 8
ATT_KV_HEADS = 2
ATT_GROUP = ATT_HEADS // ATT_KV_HEADS
KV_WIDTH = ATT_KV_HEADS * ATT_HEAD_DIM
ROT_DIM = 32
ROPE_THETA = 500000.0
RET_WIDTH = 1024
RET_HEAD_DIM = 256
RET_HEADS = 4
RET_THETA = 10000.0
D_FF = 5632
IN_COLS = ATT_WIDTH + 2 * KV_WIDTH + 4 * RET_WIDTH
EPS = 1e-6
NEG = -1e30
LOG2E = 1.4426950408889634
ATT_QSCALE = ATT_HEAD_DIM ** -0.5 * LOG2E

COL_AQ = 0
COL_AK = ATT_WIDTH
COL_AV = COL_AK + KV_WIDTH
COL_RQ = COL_AV + KV_WIDTH
COL_RK = COL_RQ + RET_WIDTH
COL_RV = COL_RK + RET_WIDTH
COL_RG = COL_RV + RET_WIDTH

VMEM_LIMIT = 56 * 1024 * 1024

TM_IN = 1408
TN_IN = 512
TM_OUT = 768
TM_FFN = 768
TF_FFN = 512
CAST_TILES = 8
EMBED_BLOCKS = 3
RET_CHUNK = 256
RET_CHUNKS = (NP - BLOCK) // RET_CHUNK
IN_CHUNK = 176
OUT_CHUNK = 128
FFN_CHUNK = 256


def _rms(x, g):
    return x * lax.rsqrt(jnp.mean(x * x, axis=-1, keepdims=True) + EPS) * g


def _valid_rows(row0, rows):
    r = row0 + lax.broadcasted_iota(jnp.int32, (rows, 1), 0)
    ok = r >= 0
    for b in range(BATCH):
        ok = ok & ~((r >= b * NP) & (r < b * NP + PAD_FRONT))
    return ok


def _embed_kernel(*refs):
    x_refs = refs[:EMBED_BLOCKS]
    meta_ref, g_ref, h_ref, u_ref = refs[EMBED_BLOCKS:]
    n = pl.program_id(1)

    def copy(x_ref, rows):
        xv = x_ref[0]
        h_ref[rows, :] = xv
        u_ref[rows, :] = _rms(xv, g_ref[...]).astype(BF16)

    @pl.when(n == 0)
    def _():
        h_ref[0:PAD_FRONT, :] = jnp.zeros((PAD_FRONT, D_MODEL), F32)
        u_ref[0:PAD_FRONT, :] = jnp.zeros((PAD_FRONT, D_MODEL), BF16)
        m = meta_ref[...]
        h_ref[PAD_FRONT:BLOCK, :] = m
        u_ref[PAD_FRONT:BLOCK, :] = _rms(m, g_ref[...]).astype(BF16)

    @pl.when(n > 0)
    def _():
        copy(x_refs[0], slice(0, BLOCK))

    for t in range(1, EMBED_BLOCKS):
        copy(x_refs[t], slice(t * BLOCK, (t + 1) * BLOCK))


def _embed(x, meta, g):
    rows = EMBED_BLOCKS * BLOCK

    def x_spec(t):
        return pl.BlockSpec(
            (1, BLOCK, D_MODEL), lambda b, n: (b, jnp.maximum(n * EMBED_BLOCKS + t - 1, 0), 0))

    return pl.pallas_call(
        _embed_kernel,
        grid=(BATCH, NB // EMBED_BLOCKS),
        in_specs=[x_spec(t) for t in range(EMBED_BLOCKS)] + [
            pl.BlockSpec((N_META, D_MODEL), lambda b, n: (0, 0)),
            pl.BlockSpec((1, D_MODEL), lambda b, n: (0, 0)),
        ],
        out_specs=[
            pl.BlockSpec((rows, D_MODEL), lambda b, n: (b * (NB // EMBED_BLOCKS) + n, 0)),
            pl.BlockSpec((rows, D_MODEL), lambda b, n: (b * (NB // EMBED_BLOCKS) + n, 0)),
        ],
        out_shape=[
            jax.ShapeDtypeStruct((M_ROWS, D_MODEL), F32),
            jax.ShapeDtypeStruct((M_ROWS, D_MODEL), BF16),
        ],
        compiler_params=pltpu.CompilerParams(
            dimension_semantics=("arbitrary", "arbitrary"),
            vmem_limit_bytes=VMEM_LIMIT),
        name="embed",
    )(*([x] * EMBED_BLOCKS), meta, g)


def _inproj_kernel(layer_ref, u_ref, w_ref, ca_ref, s1_ref, s2_ref, cr_ref, sr_ref,
                   wg_in, wu_in, wd_in, o_ref, wg_out, wu_out, wd_out):
    del layer_ref
    j = pl.program_id(1)

    def rope_att(x, r):
        return (x * ca_ref[r, :]
                + pltpu.roll(x, ROT_DIM // 2, 1) * s1_ref[r, :]
                + pltpu.roll(x, ATT_HEAD_DIM - ROT_DIM // 2, 1) * s2_ref[r, :])

    def epi_att(n_heads, scale):
        def epi(acc, r):
            for hh in range(n_heads):
                sl = slice(hh * ATT_HEAD_DIM, (hh + 1) * ATT_HEAD_DIM)
                y = rope_att(acc[:, sl], r)
                if scale != 1.0:
                    y = y * scale
                o_ref[r, sl] = y.astype(BF16)
            rest = n_heads * ATT_HEAD_DIM
            if rest < TN_IN:
                o_ref[r, rest:] = acc[:, rest:].astype(BF16)
        return epi

    def epi_ret(scale):
        def epi(acc, r):
            c = cr_ref[r, :]
            s = sr_ref[r, :]
            for hh in range(TN_IN // RET_HEAD_DIM):
                base = hh * RET_HEAD_DIM
                x1 = acc[:, base:base + 128]
                x2 = acc[:, base + 128:base + 256]
                y1 = x1 * c - x2 * s
                y2 = x2 * c + x1 * s
                if scale != 1.0:
                    y1 = y1 * scale
                    y2 = y2 * scale
                o_ref[r, base:base + 128] = y1.astype(BF16)
                o_ref[r, base + 128:base + 256] = y2.astype(BF16)
        return epi

    def epi_plain(acc, r):
        o_ref[r, :] = acc.astype(BF16)

    def run(epi):
        wg_out[...] = wg_in[...].astype(BF16)
        wu_out[...] = wu_in[...].astype(BF16)
        wd_out[...] = wd_in[...].astype(BF16)
        for k in range(TM_IN // IN_CHUNK):
            r = slice(k * IN_CHUNK, (k + 1) * IN_CHUNK)
            epi(jnp.dot(u_ref[r, :], w_ref[...], preferred_element_type=F32), r)

    n_aq = ATT_WIDTH // TN_IN
    j_akv = n_aq
    j_rq = COL_RQ // TN_IN
    j_rk = COL_RK // TN_IN
    j_rv = COL_RV // TN_IN

    @pl.when(j < n_aq)
    def _():
        run(epi_att(TN_IN // ATT_HEAD_DIM, ATT_QSCALE))

    @pl.when(j == j_akv)
    def _():
        run(epi_att(ATT_KV_HEADS, 1.0))

    @pl.when((j >= j_rq) & (j < j_rk))
    def _():
        run(epi_ret(1.0))

    @pl.when((j >= j_rk) & (j < j_rv))
    def _():
        run(epi_ret(RET_HEAD_DIM ** -0.5))

    @pl.when(j >= j_rv)
    def _():
        run(epi_plain)


def _inproj(u, w, layer, tabs, w_gate, w_up, w_down):
    tiles_per_seq = NP // TM_IN
    tab_spec = pl.BlockSpec((TM_IN, 128), lambda i, j, l: (i % tiles_per_seq, 0))
    n_i = M_ROWS // TM_IN
    n_j = IN_COLS // TN_IN
    assert n_i >= CAST_TILES and n_j * TF_FFN == D_FF
    ct = D_MODEL // CAST_TILES
    tile = lambda i: jnp.minimum(i, CAST_TILES - 1)
    return pl.pallas_call(
        _inproj_kernel,
        grid_spec=pltpu.PrefetchScalarGridSpec(
            num_scalar_prefetch=1,
            grid=(n_i, n_j),
            in_specs=[
                pl.BlockSpec((TM_IN, D_MODEL), lambda i, j, l: (i, 0)),
                pl.BlockSpec((None, D_MODEL, TN_IN), lambda i, j, l: (l[0], 0, j)),
                tab_spec, tab_spec, tab_spec, tab_spec, tab_spec,
                pl.BlockSpec((None, ct, TF_FFN), lambda i, j, l: (l[0], tile(i), j)),
                pl.BlockSpec((None, ct, TF_FFN), lambda i, j, l: (l[0], tile(i), j)),
                pl.BlockSpec((None, TF_FFN, ct), lambda i, j, l: (l[0], j, tile(i))),
            ],
            out_specs=[
                pl.BlockSpec((TM_IN, TN_IN), lambda i, j, l: (i, j)),
                pl.BlockSpec((ct, TF_FFN), lambda i, j, l: (tile(i), j)),
                pl.BlockSpec((ct, TF_FFN), lambda i, j, l: (tile(i), j)),
                pl.BlockSpec((TF_FFN, ct), lambda i, j, l: (j, tile(i))),
            ],
        ),
        out_shape=[
            jax.ShapeDtypeStruct((M_ROWS, IN_COLS), BF16),
            jax.ShapeDtypeStruct((D_MODEL, D_FF), BF16),
            jax.ShapeDtypeStruct((D_MODEL, D_FF), BF16),
            jax.ShapeDtypeStruct((D_FF, D_MODEL), BF16),
        ],
        compiler_params=pltpu.CompilerParams(
            dimension_semantics=("arbitrary", "arbitrary"),
            vmem_limit_bytes=VMEM_LIMIT),
        name="inproj",
    )(layer, u, w, *tabs, w_gate, w_up, w_down)


def _att_mask(n):
    row = lax.broadcasted_iota(jnp.int32, (BLOCK, 4 * BLOCK), 0)
    col = lax.broadcasted_iota(jnp.int32, (BLOCK, 4 * BLOCK), 1)
    qi = n * BLOCK + row
    kj = (n - 1) * BLOCK + col
    is_band = col < 3 * BLOCK
    band = is_band & (jnp.abs(qi - kj) <= WINDOW) & (kj >= PAD_FRONT) & (kj < NP)
    mj = col - 3 * BLOCK
    meta = (~is_band) & (mj >= PAD_FRONT) & (jnp.abs(qi - mj) > WINDOW)
    return band | meta


def _attn_kernel(sink_ref, q_ref, k_ref, v_ref, o_ref, bias_ref):
    kv = pl.program_id(1)
    rows = ATT_GROUP * BLOCK

    head = lax.broadcasted_iota(jnp.int32, (rows, 1), 0) // BLOCK
    sink = jnp.zeros((rows, 1), F32)
    for g in range(ATT_GROUP):
        sink = jnp.where(head == g, sink_ref[kv * ATT_GROUP + g] * LOG2E, sink)

    k_meta = k_ref[0:BLOCK, :]
    v_meta = v_ref[0:BLOCK, :]
    ones = jnp.ones((4 * BLOCK, ATT_HEAD_DIM), BF16)

    bias_ref[...] = jnp.where(_att_mask(2), 0.0, NEG).astype(F32)

    def block(n, bias, own_bias):
        r_own = pl.ds(pl.multiple_of(n * BLOCK, BLOCK), BLOCK)
        r_prev = pl.ds(pl.multiple_of(jnp.maximum(n - 1, 0) * BLOCK, BLOCK), BLOCK)
        r_next = pl.ds(pl.multiple_of(jnp.minimum(n + 1, NB - 1) * BLOCK, BLOCK), BLOCK)
        q = q_ref[r_own, :]
        qs = jnp.concatenate(
            [q[:, g * ATT_HEAD_DIM:(g + 1) * ATT_HEAD_DIM] for g in range(ATT_GROUP)], axis=0)
        ks = jnp.concatenate([k_ref[r_prev, :], k_ref[r_own, :], k_ref[r_next, :], k_meta], axis=0)
        vs = jnp.concatenate([v_ref[r_prev, :], v_ref[r_own, :], v_ref[r_next, :], v_meta], axis=0)
        vs = jnp.concatenate([vs, ones], axis=1)
        s = lax.dot_general(qs, ks, (((1,), (1,)), ((), ())), preferred_element_type=F32)
        bias4 = jnp.concatenate([bias] * ATT_GROUP, axis=0)
        if own_bias:
            s = s + bias4
        else:
            s = jnp.concatenate([s[:, :BLOCK] + bias4[:, :BLOCK], s[:, BLOCK:2 * BLOCK],
                                 s[:, 2 * BLOCK:] + bias4[:, 2 * BLOCK:]], axis=1)
        m = jnp.maximum(jnp.max(s, axis=-1, keepdims=True), sink)
        p = jnp.exp2(s - m).astype(BF16)
        o = jnp.dot(p, vs, preferred_element_type=F32)
        o = o[:, :ATT_HEAD_DIM] / (o[:, ATT_HEAD_DIM:] + jnp.exp2(sink - m))
        for g in range(ATT_GROUP):
            o_ref[r_own, g * ATT_HEAD_DIM:(g + 1) * ATT_HEAD_DIM] = (
                o[g * BLOCK:(g + 1) * BLOCK, :].astype(BF16))

    def edge(n):
        block(n, jnp.where(_att_mask(n), 0.0, NEG).astype(F32), True)

    edge(0)
    edge(1)

    def body(n, carry):
        block(n, bias_ref[...], False)
        return carry

    lax.fori_loop(2, NB - 1, body, 0, unroll=10)
    edge(NB - 1)


def _attention(proj, sink):
    q_blk = ATT_GROUP * ATT_HEAD_DIM
    return pl.pallas_call(
        _attn_kernel,
        grid=(BATCH, ATT_KV_HEADS),
        in_specs=[
            pl.BlockSpec(memory_space=pltpu.SMEM),
            pl.BlockSpec((NP, q_blk), lambda b, k: (b, k)),
            pl.BlockSpec((NP, ATT_HEAD_DIM), lambda b, k: (b, COL_AK // ATT_HEAD_DIM + k)),
            pl.BlockSpec((NP, ATT_HEAD_DIM), lambda b, k: (b, COL_AV // ATT_HEAD_DIM + k)),
        ],
        out_specs=pl.BlockSpec((NP, q_blk), lambda b, k: (b, k)),
        out_shape=jax.ShapeDtypeStruct((M_ROWS, ATT_WIDTH), BF16),
        scratch_shapes=[pltpu.VMEM((BLOCK, 4 * BLOCK), F32)],
        compiler_params=pltpu.CompilerParams(
            dimension_semantics=("arbitrary", "arbitrary"),
            vmem_limit_bytes=VMEM_LIMIT),
        name="attention",
    )(sink, proj, proj, proj)


def _ret_kernel(dec_ref, q_ref, k_ref, v_ref, g_ref, gain_ref, o_ref, st_ref, sf_ref, sb_ref):
    hd = pl.program_id(1)
    lgf = -jnp.exp(jnp.full((1, RET_HEAD_DIM), dec_ref[0, hd], F32))
    lgb = -jnp.exp(jnp.full((1, RET_HEAD_DIM), dec_ref[1, hd], F32))

    def weights(L):
        idx = lax.broadcasted_iota(jnp.int32, (L, RET_HEAD_DIM), 0).astype(F32)
        r = lax.broadcasted_iota(jnp.int32, (L, L), 0)
        c = lax.broadcasted_iota(jnp.int32, (L, L), 1)
        diff = (r - c).astype(F32)
        return dict(
            xi_f=jnp.exp(lgf * (idx + 1.0)).astype(BF16),
            zeta_f=jnp.exp(lgf * (L - 1.0 - idx)).astype(BF16),
            xi_b=jnp.exp(lgb * (L - idx)).astype(BF16),
            zeta_b=jnp.exp(lgb * idx).astype(BF16),
            dec_f=jnp.exp(lgf * float(L)),
            dec_b=jnp.exp(lgb * float(L)),
            dmask=jnp.where(diff >= 0.0,
                            jnp.exp(lgf[:, :L] * jnp.maximum(diff, 0.0)),
                            jnp.exp(lgb[:, :L] * jnp.maximum(-diff, 0.0))))

    w_head = weights(BLOCK)
    w_main = weights(RET_CHUNK)

    def main_rows(c):
        return pl.ds(pl.multiple_of(BLOCK + (c - 1) * RET_CHUNK, BLOCK), RET_CHUNK)

    head_rows = slice(0, BLOCK)

    def kv_outer(rows, zeta):
        return lax.dot_general(k_ref[rows, :] * zeta, v_ref[rows, :], (((0,), (0,)), ((), ())),
                               preferred_element_type=F32)

    zeros = jnp.zeros((RET_HEAD_DIM, RET_HEAD_DIM), F32)
    st_ref[0, 0:RET_HEAD_DIM, :] = zeros.astype(BF16)
    sf_ref[...] = kv_outer(head_rows, w_head["zeta_f"])
    sb_ref[...] = zeros

    def scan(t, carry):
        cf = 1 + t
        cb = RET_CHUNKS - t
        st_ref[cf, 0:RET_HEAD_DIM, :] = sf_ref[...].astype(BF16)
        st_ref[cb, RET_HEAD_DIM:, :] = sb_ref[...].astype(BF16)
        sf_ref[...] = sf_ref[...] * w_main["dec_f"] + kv_outer(main_rows(cf), w_main["zeta_f"])
        sb_ref[...] = sb_ref[...] * w_main["dec_b"] + kv_outer(main_rows(cb), w_main["zeta_b"])
        return carry

    lax.fori_loop(0, RET_CHUNKS, scan, 0, unroll=2)
    st_ref[0, RET_HEAD_DIM:, :] = sb_ref[...].astype(BF16)

    gain = gain_ref[...]

    def out(rows, c, w):
        q = q_ref[rows, :]
        k = k_ref[rows, :]
        v = v_ref[rows, :]
        a = lax.dot_general(q, k, (((1,), (1,)), ((), ())), preferred_element_type=F32)
        p = (a * w["dmask"]).astype(BF16)
        qc = jnp.concatenate([q * w["xi_f"], q * w["xi_b"]], axis=1)
        y = (jnp.dot(p, v, preferred_element_type=F32)
             + jnp.dot(qc, st_ref[c], preferred_element_type=F32))
        y = _rms(y, gain)
        gate = g_ref[rows, :].astype(F32)
        gate = gate * (1.0 / (1.0 + jnp.exp(-gate)))
        o_ref[rows, :] = (gate * y).astype(BF16)

    out(head_rows, 0, w_head)

    def out_main(t, carry):
        c = 1 + t
        out(main_rows(c), c, w_main)
        return carry

    lax.fori_loop(0, RET_CHUNKS, out_main, 0, unroll=16)


def _retention(proj, dec, gain):
    def col_spec(col0):
        return pl.BlockSpec((NP, RET_HEAD_DIM), lambda b, h: (b, col0 // RET_HEAD_DIM + h))

    return pl.pallas_call(
        _ret_kernel,
        grid=(BATCH, RET_HEADS),
        in_specs=[
            pl.BlockSpec(memory_space=pltpu.SMEM),
            col_spec(COL_RQ), col_spec(COL_RK), col_spec(COL_RV), col_spec(COL_RG),
            pl.BlockSpec((1, RET_HEAD_DIM), lambda b, h: (0, h)),
        ],
        out_specs=pl.BlockSpec((NP, RET_HEAD_DIM), lambda b, h: (b, h)),
        out_shape=jax.ShapeDtypeStruct((M_ROWS, RET_WIDTH), BF16),
        scratch_shapes=[
            pltpu.VMEM((RET_CHUNKS + 1, 2 * RET_HEAD_DIM, RET_HEAD_DIM), BF16),
            pltpu.VMEM((RET_HEAD_DIM, RET_HEAD_DIM), F32),
            pltpu.VMEM((RET_HEAD_DIM, RET_HEAD_DIM), F32),
        ],
        compiler_params=pltpu.CompilerParams(
            dimension_semantics=("arbitrary", "arbitrary"),
            vmem_limit_bytes=VMEM_LIMIT),
        name="retention",
    )(dec, proj, proj, proj, proj, gain)


def _post_norms(y, h_rows, gpost, gnext, valid):
    hn = h_rows + _rms(y, gpost)
    un = _rms(hn, gnext)
    if valid is not None:
        un = jnp.where(valid, un, 0.0)
    return hn, un.astype(BF16)


def _outproj_kernel(att_ref, ret_ref, w_ref, gpost_ref, gffn_ref, h_ref, ho_ref, u_ref):
    n_chunks = TM_OUT // OUT_CHUNK
    rows = lambda k: slice(k * OUT_CHUNK, (k + 1) * OUT_CHUNK)

    def project(k):
        r = rows(k)
        return (jnp.dot(att_ref[r, :], w_ref[0:ATT_WIDTH, :], preferred_element_type=F32)
                + jnp.dot(ret_ref[r, :], w_ref[ATT_WIDTH:, :], preferred_element_type=F32))

    y = project(0)
    for k in range(n_chunks):
        y_next = project(k + 1) if k + 1 < n_chunks else None
        hn, un = _post_norms(y, h_ref[rows(k), :], gpost_ref[...], gffn_ref[...], None)
        ho_ref[rows(k), :] = hn
        u_ref[rows(k), :] = un
        y = y_next


def _outproj(att, ret, w, layer, gpost, gffn, h):
    row = lambda i: (i, 0)
    const = lambda i: (0, 0)
    return pl.pallas_call(
        _outproj_kernel,
        grid=(M_ROWS // TM_OUT,),
        in_specs=[
            pl.BlockSpec((TM_OUT, ATT_WIDTH), row),
            pl.BlockSpec((TM_OUT, RET_WIDTH), row),
            pl.BlockSpec((None, D_MODEL, D_MODEL), lambda i: (layer, 0, 0)),
            pl.BlockSpec((1, D_MODEL), const),
            pl.BlockSpec((1, D_MODEL), const),
            pl.BlockSpec((TM_OUT, D_MODEL), row),
        ],
        out_specs=[
            pl.BlockSpec((TM_OUT, D_MODEL), row),
            pl.BlockSpec((TM_OUT, D_MODEL), row),
        ],
        out_shape=[
            jax.ShapeDtypeStruct((M_ROWS, D_MODEL), F32),
            jax.ShapeDtypeStruct((M_ROWS, D_MODEL), BF16),
        ],
        input_output_aliases={5: 0},
        compiler_params=pltpu.CompilerParams(
            dimension_semantics=("arbitrary",),
            vmem_limit_bytes=VMEM_LIMIT),
        name="outproj",
    )(att, ret, w, gpost, gffn, h)


def _ffn_kernel(u_ref, wg_ref, wu_ref, wd_ref, gpost_ref, gnext_ref, h_ref, ho_ref, *maybe_un_ref):
    i = pl.program_id(0)
    j = pl.program_id(1)
    last = pl.num_programs(1) - 1

    def swiglu(rows):
        u = u_ref[rows, :]
        g = jnp.dot(u, wg_ref[...], preferred_element_type=F32)
        up = jnp.dot(u, wu_ref[...], preferred_element_type=F32)
        f = (g * (1.0 / (1.0 + jnp.exp(-g))) * up).astype(BF16)
        return jnp.dot(f, wd_ref[...], preferred_element_type=F32)

    @pl.when(j == 0)
    def _():
        ho_ref[...] = jnp.zeros((TM_FFN, D_MODEL), F32)

    @pl.when(j < last)
    def _():
        ho_ref[...] += swiglu(slice(None))

    @pl.when(j == last)
    def _():
        n_chunks = TM_FFN // FFN_CHUNK
        rows = lambda k: slice(k * FFN_CHUNK, (k + 1) * FFN_CHUNK)
        y_next = ho_ref[rows(0), :] + swiglu(rows(0))
        for k in range(n_chunks):
            r = rows(k)
            y = y_next
            if k + 1 < n_chunks:
                y_next = ho_ref[rows(k + 1), :] + swiglu(rows(k + 1))
            if maybe_un_ref:
                valid = _valid_rows(i * TM_FFN + k * FFN_CHUNK, FFN_CHUNK)
                hn, un = _post_norms(y, h_ref[r, :], gpost_ref[...], gnext_ref[...], valid)
                maybe_un_ref[0][r, :] = un
            else:
                hn = h_ref[r, :] + _rms(y, gpost_ref[...])
            ho_ref[r, :] = hn


def _ffn(u, wg, wu, wd, gpost, gnext, h, emit_next):
    row = lambda i, j: (i, 0)
    const = lambda i, j: (0, 0)
    out_specs = [pl.BlockSpec((TM_FFN, D_MODEL), row)]
    out_shape = [jax.ShapeDtypeStruct((M_ROWS, D_MODEL), F32)]
    if emit_next:
        out_specs.append(pl.BlockSpec((TM_FFN, D_MODEL), row))
        out_shape.append(jax.ShapeDtypeStruct((M_ROWS, D_MODEL), BF16))
    return pl.pallas_call(
        _ffn_kernel,
        grid=(M_ROWS // TM_FFN, D_FF // TF_FFN),
        in_specs=[
            pl.BlockSpec((TM_FFN, D_MODEL), row),
            pl.BlockSpec((D_MODEL, TF_FFN), lambda i, j: (0, j)),
            pl.BlockSpec((D_MODEL, TF_FFN), lambda i, j: (0, j)),
            pl.BlockSpec((TF_FFN, D_MODEL), lambda i, j: (j, 0)),
            pl.BlockSpec((1, D_MODEL), const),
            pl.BlockSpec((1, D_MODEL), const),
            pl.BlockSpec((TM_FFN, D_MODEL), row),
        ],
        out_specs=out_specs,
        out_shape=out_shape,
        input_output_aliases={6: 0},
        compiler_params=pltpu.CompilerParams(
            dimension_semantics=("arbitrary", "arbitrary"),
            vmem_limit_bytes=VMEM_LIMIT),
        name="ffn",
    )(u, wg, wu, wd, gpost, gnext, h)


def _rope_tables():
    pos = (jnp.arange(NP) - PAD_FRONT).astype(F32)
    half_a = ROT_DIM // 2
    inv_a = ROPE_THETA ** (-jnp.arange(half_a, dtype=F32) / half_a)
    ang_a = pos[:, None] * inv_a[None, :]
    cos_a, sin_a = jnp.cos(ang_a), jnp.sin(ang_a)
    rest = ATT_HEAD_DIM - ROT_DIM
    ca = jnp.concatenate([cos_a, cos_a, jnp.ones((NP, rest), F32)], axis=1)
    s1 = jnp.concatenate([jnp.zeros((NP, half_a), F32), sin_a, jnp.zeros((NP, rest), F32)], axis=1)
    s2 = jnp.concatenate([-sin_a, jnp.zeros((NP, ATT_HEAD_DIM - half_a), F32)], axis=1)
    half_r = RET_HEAD_DIM // 2
    inv_r = RET_THETA ** (-jnp.arange(half_r, dtype=F32) / half_r)
    ang_r = pos[:, None] * inv_r[None, :]
    return ca, s1, s2, jnp.cos(ang_r), jnp.sin(ang_r)


def kernel(x, meta_tokens, w_in, w_out, attn_sink, ret_decay_fwd, ret_decay_bwd, ret_norm,
           norm_mix_pre, norm_mix_post, w_gate, w_up, w_down, norm_ffn_pre, norm_ffn_post):
    tabs = _rope_tables()
    row = lambda a, l: a[l].reshape(1, -1).astype(F32)
    w_in = w_in.astype(BF16)
    w_out = w_out.astype(BF16)
    w_gate, w_up, w_down = (w.astype(F32) for w in (w_gate, w_up, w_down))
    h, u = _embed(x, meta_tokens, row(norm_mix_pre, 0))
    for l in range(DEPTH):
        layer = jnp.full((1,), l, jnp.int32)
        proj, wg, wu, wd = _inproj(u, w_in, layer, tabs, w_gate, w_up, w_down)
        att = _attention(proj, attn_sink[l].astype(F32))
        dec = jnp.stack([ret_decay_fwd[l], ret_decay_bwd[l]]).astype(F32)
        ret = _retention(proj, dec, row(ret_norm, l))
        h, u = _outproj(att, ret, w_out, l, row(norm_mix_post, l), row(norm_ffn_pre, l), h)
        emit_next = l + 1 < DEPTH
        gnext = row(norm_mix_pre, l + 1) if emit_next else row(norm_mix_pre, l)
        res = _ffn(u, wg, wu, wd, row(norm_ffn_post, l), gnext, h, emit_next)
        h, u = res if emit_next else (res[0], None)
    return h.reshape(BATCH, NP, D_MODEL)[:, BLOCK:]
```

```python
import functools

import jax
import jax.numpy as jnp
from jax import lax
from jax.experimental import pallas as pl
from jax.experimental.pallas import tpu as pltpu

F32 = jnp.float32
BF16 = jnp.bfloat16

D_MODEL = 2048
BATCH = 4
SEQ = 4096
DEPTH = 4
N_META = 16
BLOCK = 128
WINDOW = 128
PAD_FRONT = BLOCK - N_META
NP = PAD_FRONT + N_META + SEQ
NB = NP // BLOCK
M_ROWS = BATCH * NP
ATT_HEAD_DIM = 128
ATT_WIDTH = 1024
ATT_HEADS = 8
ATT_KV_HEADS = 2
ATT_GROUP = ATT_HEADS // ATT_KV_HEADS
KV_WIDTH = ATT_KV_HEADS * ATT_HEAD_DIM
ROT_DIM = 32
ROPE_THETA = 500000.0
RET_WIDTH = 1024
RET_HEAD_DIM = 256
RET_HEADS = 4
RET_THETA = 10000.0
D_FF = 5632
IN_COLS = ATT_WIDTH + 2 * KV_WIDTH + 4 * RET_WIDTH
EPS = 1e-6
NEG = -1e30
LOG2E = 1.4426950408889634
ATT_QSCALE = ATT_HEAD_DIM ** -0.5 * LOG2E

COL_AQ = 0
COL_AK = ATT_WIDTH
COL_AV = COL_AK + KV_WIDTH
COL_RQ = COL_AV + KV_WIDTH
COL_RK = COL_RQ + RET_WIDTH
COL_RV = COL_RK + RET_WIDTH
COL_RG = COL_RV + RET_WIDTH

VMEM_LIMIT = 56 * 1024 * 1024

TM_IN = 1408
TN_IN = 512
TM_OUT = 768
TM_FFN = 768
TF_FFN = 512
IN_CAST_TILES = 16
CAST_TILES = 8
EMBED_BLOCKS = 3
RET_CHUNK = 256
RET_CHUNKS = (NP - BLOCK) // RET_CHUNK
IN_CHUNK = 176
OUT_CHUNK = 128
FFN_CHUNK = 256


def _rms(x, g):
    return x * lax.rsqrt(jnp.mean(x * x, axis=-1, keepdims=True) + EPS) * g


def _valid_rows(row0, rows):
    r = row0 + lax.broadcasted_iota(jnp.int32, (rows, 1), 0)
    ok = r >= 0
    for b in range(BATCH):
        ok = ok & ~((r >= b * NP) & (r < b * NP + PAD_FRONT))
    return ok


def _embed_kernel(*refs):
    x_refs = refs[:EMBED_BLOCKS]
    meta_ref, g_ref, w_ref, h_ref, u_ref, wb_ref = refs[EMBED_BLOCKS:]
    n = pl.program_id(1)
    wb_ref[...] = w_ref[...].astype(BF16)

    def copy(x_ref, rows):
        xv = x_ref[0]
        h_ref[rows, :] = xv
        u_ref[rows, :] = _rms(xv, g_ref[...]).astype(BF16)

    @pl.when(n == 0)
    def _():
        h_ref[0:PAD_FRONT, :] = jnp.zeros((PAD_FRONT, D_MODEL), F32)
        u_ref[0:PAD_FRONT, :] = jnp.zeros((PAD_FRONT, D_MODEL), BF16)
        m = meta_ref[...]
        h_ref[PAD_FRONT:BLOCK, :] = m
        u_ref[PAD_FRONT:BLOCK, :] = _rms(m, g_ref[...]).astype(BF16)

    @pl.when(n > 0)
    def _():
        copy(x_refs[0], slice(0, BLOCK))

    for t in range(1, EMBED_BLOCKS):
        copy(x_refs[t], slice(t * BLOCK, (t + 1) * BLOCK))


def _embed(x, meta, g, w_in):
    rows = EMBED_BLOCKS * BLOCK
    steps = NB // EMBED_BLOCKS
    wr, wc = D_MODEL // BATCH, IN_COLS // steps
    assert wr * BATCH == D_MODEL and wc * steps == IN_COLS

    def x_spec(t):
        return pl.BlockSpec(
            (1, BLOCK, D_MODEL), lambda b, n: (b, jnp.maximum(n * EMBED_BLOCKS + t - 1, 0), 0))

    return pl.pallas_call(
        _embed_kernel,
        grid=(BATCH, steps),
        in_specs=[x_spec(t) for t in range(EMBED_BLOCKS)] + [
            pl.BlockSpec((N_META, D_MODEL), lambda b, n: (0, 0)),
            pl.BlockSpec((1, D_MODEL), lambda b, n: (0, 0)),
            pl.BlockSpec((None, wr, wc), lambda b, n: (0, b, n)),
        ],
        out_specs=[
            pl.BlockSpec((rows, D_MODEL), lambda b, n: (b * steps + n, 0)),
            pl.BlockSpec((rows, D_MODEL), lambda b, n: (b * steps + n, 0)),
            pl.BlockSpec((wr, wc), lambda b, n: (b, n)),
        ],
        out_shape=[
            jax.ShapeDtypeStruct((M_ROWS, D_MODEL), F32),
            jax.ShapeDtypeStruct((M_ROWS, D_MODEL), BF16),
            jax.ShapeDtypeStruct((D_MODEL, IN_COLS), BF16),
        ],
        compiler_params=pltpu.CompilerParams(
            dimension_semantics=("arbitrary", "arbitrary"),
            vmem_limit_bytes=VMEM_LIMIT),
        name="embed",
    )(*([x] * EMBED_BLOCKS), meta, g, w_in)


def _inproj_kernel(layer_ref, u_ref, w_ref, ca_ref, s1_ref, s2_ref, cr_ref, sr_ref,
                   wg_in, wu_in, wd_in, o_ref, wg_out, wu_out, wd_out):
    del layer_ref
    j = pl.program_id(1)
    tab_row0 = (pl.program_id(0) % (NP // TM_IN)) * TM_IN

    def tab_rows(r):
        return pl.ds(pl.multiple_of(tab_row0 + r.start, 8), r.stop - r.start)

    def rope_att(x, r):
        tr = tab_rows(r)
        return (x * ca_ref[tr, :]
                + pltpu.roll(x, ROT_DIM // 2, 1) * s1_ref[tr, :]
                + pltpu.roll(x, ATT_HEAD_DIM - ROT_DIM // 2, 1) * s2_ref[tr, :])

    def epi_att(n_heads, scale):
        def epi(acc, r):
            for hh in range(n_heads):
                sl = slice(hh * ATT_HEAD_DIM, (hh + 1) * ATT_HEAD_DIM)
                y = rope_att(acc[:, sl], r)
                if scale != 1.0:
                    y = y * scale
                o_ref[r, sl] = y.astype(BF16)
            rest = n_heads * ATT_HEAD_DIM
            if rest < TN_IN:
                o_ref[r, rest:] = acc[:, rest:].astype(BF16)
        return epi

    def epi_ret(scale):
        def epi(acc, r):
            c = cr_ref[tab_rows(r), :]
            s = sr_ref[tab_rows(r), :]
            for hh in range(TN_IN // RET_HEAD_DIM):
                base = hh * RET_HEAD_DIM
                x1 = acc[:, base:base + 128]
                x2 = acc[:, base + 128:base + 256]
                y1 = x1 * c - x2 * s
                y2 = x2 * c + x1 * s
                if scale != 1.0:
                    y1 = y1 * scale
                    y2 = y2 * scale
                o_ref[r, base:base + 128] = y1.astype(BF16)
                o_ref[r, base + 128:base + 256] = y2.astype(BF16)
        return epi

    def epi_plain(acc, r):
        o_ref[r, :] = acc.astype(BF16)

    def run(epi):
        wg_out[...] = wg_in[...].astype(BF16)
        wu_out[...] = wu_in[...].astype(BF16)
        wd_out[...] = wd_in[...].astype(BF16)
        for k in range(TM_IN // IN_CHUNK):
            r = slice(k * IN_CHUNK, (k + 1) * IN_CHUNK)
            epi(jnp.dot(u_ref[r, :], w_ref[...], preferred_element_type=F32), r)

    n_aq = ATT_WIDTH // TN_IN
    j_akv = n_aq
    j_rq = COL_RQ // TN_IN
    j_rk = COL_RK // TN_IN
    j_rv = COL_RV // TN_IN

    @pl.when(j < n_aq)
    def _():
        run(epi_att(TN_IN // ATT_HEAD_DIM, ATT_QSCALE))

    @pl.when(j == j_akv)
    def _():
        run(epi_att(ATT_KV_HEADS, 1.0))

    @pl.when((j >= j_rq) & (j < j_rk))
    def _():
        run(epi_ret(1.0))

    @pl.when((j >= j_rk) & (j < j_rv))
    def _():
        run(epi_ret(RET_HEAD_DIM ** -0.5))

    @pl.when(j >= j_rv)
    def _():
        run(epi_plain)


def _inproj(u, w, layer, tabs, w_gate, w_up, w_down):
    tab_spec = pl.BlockSpec((NP, 128), lambda i, j, l: (0, 0))
    n_i = M_ROWS // TM_IN
    n_j = IN_COLS // TN_IN
    assert n_i >= CAST_TILES and n_j * TF_FFN == D_FF
    ct = D_MODEL // CAST_TILES
    tile = lambda i: jnp.minimum(i, CAST_TILES - 1)
    return pl.pallas_call(
        _inproj_kernel,
        grid_spec=pltpu.PrefetchScalarGridSpec(
            num_scalar_prefetch=1,
            grid=(n_i, n_j),
            in_specs=[
                pl.BlockSpec((TM_IN, D_MODEL), lambda i, j, l: (i, 0)),
                pl.BlockSpec((D_MODEL, TN_IN), lambda i, j, l: (0, j)),
                tab_spec, tab_spec, tab_spec, tab_spec, tab_spec,
                pl.BlockSpec((None, ct, TF_FFN), lambda i, j, l: (l[0], tile(i), j)),
                pl.BlockSpec((None, ct, TF_FFN), lambda i, j, l: (l[0], tile(i), j)),
                pl.BlockSpec((None, TF_FFN, ct), lambda i, j, l: (l[0], j, tile(i))),
            ],
            out_specs=[
                pl.BlockSpec((TM_IN, TN_IN), lambda i, j, l: (i, j)),
                pl.BlockSpec((ct, TF_FFN), lambda i, j, l: (i, j)),
                pl.BlockSpec((ct, TF_FFN), lambda i, j, l: (i, j)),
                pl.BlockSpec((TF_FFN, ct), lambda i, j, l: (j, i)),
            ],
        ),
        out_shape=[
            jax.ShapeDtypeStruct((M_ROWS, IN_COLS), BF16),
            jax.ShapeDtypeStruct((n_i * ct, D_FF), BF16),
            jax.ShapeDtypeStruct((n_i * ct, D_FF), BF16),
            jax.ShapeDtypeStruct((D_FF, n_i * ct), BF16),
        ],
        compiler_params=pltpu.CompilerParams(
            dimension_semantics=("arbitrary", "arbitrary"),
            vmem_limit_bytes=VMEM_LIMIT),
        name="inproj",
    )(layer, u, w, *tabs, w_gate, w_up, w_down)


def _att_mask(n):
    row = lax.broadcasted_iota(jnp.int32, (BLOCK, 4 * BLOCK), 0)
    col = lax.broadcasted_iota(jnp.int32, (BLOCK, 4 * BLOCK), 1)
    qi = n * BLOCK + row
    kj = (n - 1) * BLOCK + col
    is_band = col < 3 * BLOCK
    band = is_band & (jnp.abs(qi - kj) <= WINDOW) & (kj >= PAD_FRONT) & (kj < NP)
    mj = col - 3 * BLOCK
    meta = (~is_band) & (mj >= PAD_FRONT) & (jnp.abs(qi - mj) > WINDOW)
    return band | meta


def _attn_kernel(sink_ref, q_ref, k_ref, v_ref, o_ref, bias_ref):
    kv = pl.program_id(1)
    rows = ATT_GROUP * BLOCK

    head = lax.broadcasted_iota(jnp.int32, (rows, 1), 0) // BLOCK
    sink = jnp.zeros((rows, 1), F32)
    for g in range(ATT_GROUP):
        sink = jnp.where(head == g, sink_ref[kv * ATT_GROUP + g] * LOG2E, sink)

    k_meta = k_ref[0:BLOCK, :]
    v_meta = v_ref[0:BLOCK, :]
    ones = jnp.ones((4 * BLOCK, ATT_HEAD_DIM), BF16)

    bias_ref[...] = jnp.where(_att_mask(2), 0.0, NEG).astype(F32)

    def block(n, bias, own_bias):
        r_own = pl.ds(pl.multiple_of(n * BLOCK, BLOCK), BLOCK)
        r_prev = pl.ds(pl.multiple_of(jnp.maximum(n - 1, 0) * BLOCK, BLOCK), BLOCK)
        r_next = pl.ds(pl.multiple_of(jnp.minimum(n + 1, NB - 1) * BLOCK, BLOCK), BLOCK)
        q = q_ref[r_own, :]
        qs = jnp.concatenate(
            [q[:, g * ATT_HEAD_DIM:(g + 1) * ATT_HEAD_DIM] for g in range(ATT_GROUP)], axis=0)
        ks = jnp.concatenate([k_ref[r_prev, :], k_ref[r_own, :], k_ref[r_next, :], k_meta], axis=0)
        vs = jnp.concatenate([v_ref[r_prev, :], v_ref[r_own, :], v_ref[r_next, :], v_meta], axis=0)
        vs = jnp.concatenate([vs, ones], axis=1)
        s = lax.dot_general(qs, ks, (((1,), (1,)), ((), ())), preferred_element_type=F32)
        bias4 = jnp.concatenate([bias] * ATT_GROUP, axis=0)
        if own_bias:
            s = s + bias4
        else:
            s = jnp.concatenate([s[:, :BLOCK] + bias4[:, :BLOCK], s[:, BLOCK:2 * BLOCK],
                                 s[:, 2 * BLOCK:] + bias4[:, 2 * BLOCK:]], axis=1)
        m = jnp.maximum(jnp.max(s, axis=-1, keepdims=True), sink)
        p = jnp.exp2(s - m).astype(BF16)
        o = jnp.dot(p, vs, preferred_element_type=F32)
        o = o[:, :ATT_HEAD_DIM] / (o[:, ATT_HEAD_DIM:] + jnp.exp2(sink - m))
        for g in range(ATT_GROUP):
            o_ref[r_own, g * ATT_HEAD_DIM:(g + 1) * ATT_HEAD_DIM] = (
                o[g * BLOCK:(g + 1) * BLOCK, :].astype(BF16))

    def edge(n):
        block(n, jnp.where(_att_mask(n), 0.0, NEG).astype(F32), True)

    edge(0)
    edge(1)

    def body(n, carry):
        block(n, bias_ref[...], False)
        return carry

    lax.fori_loop(2, NB - 1, body, 0, unroll=10)
    edge(NB - 1)


def _attention(proj, sink):
    q_blk = ATT_GROUP * ATT_HEAD_DIM
    return pl.pallas_call(
        _attn_kernel,
        grid=(BATCH, ATT_KV_HEADS),
        in_specs=[
            pl.BlockSpec(memory_space=pltpu.SMEM),
            pl.BlockSpec((NP, q_blk), lambda b, k: (b, k)),
            pl.BlockSpec((NP, ATT_HEAD_DIM), lambda b, k: (b, COL_AK // ATT_HEAD_DIM + k)),
            pl.BlockSpec((NP, ATT_HEAD_DIM), lambda b, k: (b, COL_AV // ATT_HEAD_DIM + k)),
        ],
        out_specs=pl.BlockSpec((NP, q_blk), lambda b, k: (b, k)),
        out_shape=jax.ShapeDtypeStruct((M_ROWS, ATT_WIDTH), BF16),
        scratch_shapes=[pltpu.VMEM((BLOCK, 4 * BLOCK), F32)],
        compiler_params=pltpu.CompilerParams(
            dimension_semantics=("arbitrary", "arbitrary"),
            vmem_limit_bytes=VMEM_LIMIT),
        name="attention",
    )(sink, proj, proj, proj)


def _ret_kernel(dec_ref, q_ref, k_ref, v_ref, g_ref, gain_ref, wo_ref,
                o_ref, wob_ref, st_ref, sf_ref, sb_ref):
    wob_ref[...] = wo_ref[...].astype(BF16)
    hd = pl.program_id(1)
    lgf = -jnp.exp(jnp.full((1, RET_HEAD_DIM), dec_ref[0, hd], F32))
    lgb = -jnp.exp(jnp.full((1, RET_HEAD_DIM), dec_ref[1, hd], F32))

    def weights(L):
        idx = lax.broadcasted_iota(jnp.int32, (L, RET_HEAD_DIM), 0).astype(F32)
        r = lax.broadcasted_iota(jnp.int32, (L, L), 0)
        c = lax.broadcasted_iota(jnp.int32, (L, L), 1)
        diff = (r - c).astype(F32)
        return dict(
            xi_f=jnp.exp(lgf * (idx + 1.0)).astype(BF16),
            zeta_f=jnp.exp(lgf * (L - 1.0 - idx)).astype(BF16),
            xi_b=jnp.exp(lgb * (L - idx)).astype(BF16),
            zeta_b=jnp.exp(lgb * idx).astype(BF16),
            dec_f=jnp.exp(lgf * float(L)),
            dec_b=jnp.exp(lgb * float(L)),
            dmask=jnp.where(diff >= 0.0,
                            jnp.exp(lgf[:, :L] * jnp.maximum(diff, 0.0)),
                            jnp.exp(lgb[:, :L] * jnp.maximum(-diff, 0.0))))

    w_head = weights(BLOCK)
    w_main = weights(RET_CHUNK)

    def main_rows(c):
        return pl.ds(pl.multiple_of(BLOCK + (c - 1) * RET_CHUNK, BLOCK), RET_CHUNK)

    head_rows = slice(0, BLOCK)

    def kv_outer(rows, zeta):
        return lax.dot_general(k_ref[rows, :] * zeta, v_ref[rows, :], (((0,), (0,)), ((), ())),
                               preferred_element_type=F32)

    zeros = jnp.zeros((RET_HEAD_DIM, RET_HEAD_DIM), F32)
    st_ref[0, 0:RET_HEAD_DIM, :] = zeros.astype(BF16)
    sf_ref[...] = kv_outer(head_rows, w_head["zeta_f"])
    sb_ref[...] = zeros

    def scan(t, carry):
        cf = 1 + t
        cb = RET_CHUNKS - t
        st_ref[cf, 0:RET_HEAD_DIM, :] = sf_ref[...].astype(BF16)
        st_ref[cb, RET_HEAD_DIM:, :] = sb_ref[...].astype(BF16)
        sf_ref[...] = sf_ref[...] * w_main["dec_f"] + kv_outer(main_rows(cf), w_main["zeta_f"])
        sb_ref[...] = sb_ref[...] * w_main["dec_b"] + kv_outer(main_rows(cb), w_main["zeta_b"])
        return carry

    lax.fori_loop(0, RET_CHUNKS, scan, 0, unroll=2)
    st_ref[0, RET_HEAD_DIM:, :] = sb_ref[...].astype(BF16)

    gain = gain_ref[...]

    def out(rows, c, w):
        q = q_ref[rows, :]
        k = k_ref[rows, :]
        v = v_ref[rows, :]
        a = lax.dot_general(q, k, (((1,), (1,)), ((), ())), preferred_element_type=F32)
        p = (a * w["dmask"]).astype(BF16)
        qc = jnp.concatenate([q * w["xi_f"], q * w["xi_b"]], axis=1)
        y = (jnp.dot(p, v, preferred_element_type=F32)
             + jnp.dot(qc, st_ref[c], preferred_element_type=F32))
        y = _rms(y, gain)
        gate = g_ref[rows, :].astype(F32)
        gate = gate * (1.0 / (1.0 + jnp.exp(-gate)))
        o_ref[rows, :] = (gate * y).astype(BF16)

    out(head_rows, 0, w_head)

    def out_main(t, carry):
        c = 1 + t
        out(main_rows(c), c, w_main)
        return carry

    lax.fori_loop(0, RET_CHUNKS, out_main, 0, unroll=16)


def _retention(proj, dec, gain, w_out, layer):
    def col_spec(col0):
        return pl.BlockSpec((NP, RET_HEAD_DIM), lambda b, h: (b, col0 // RET_HEAD_DIM + h))

    wr = D_MODEL // (BATCH * RET_HEADS)
    return pl.pallas_call(
        _ret_kernel,
        grid=(BATCH, RET_HEADS),
        in_specs=[
            pl.BlockSpec(memory_space=pltpu.SMEM),
            col_spec(COL_RQ), col_spec(COL_RK), col_spec(COL_RV), col_spec(COL_RG),
            pl.BlockSpec((1, RET_HEAD_DIM), lambda b, h: (0, h)),
            pl.BlockSpec((None, wr, D_MODEL), lambda b, h: (layer, b * RET_HEADS + h, 0)),
        ],
        out_specs=[
            pl.BlockSpec((NP, RET_HEAD_DIM), lambda b, h: (b, h)),
            pl.BlockSpec((wr, D_MODEL), lambda b, h: (b * RET_HEADS + h, 0)),
        ],
        out_shape=[
            jax.ShapeDtypeStruct((M_ROWS, RET_WIDTH), BF16),
            jax.ShapeDtypeStruct((D_MODEL, D_MODEL), BF16),
        ],
        scratch_shapes=[
            pltpu.VMEM((RET_CHUNKS + 1, 2 * RET_HEAD_DIM, RET_HEAD_DIM), BF16),
            pltpu.VMEM((RET_HEAD_DIM, RET_HEAD_DIM), F32),
            pltpu.VMEM((RET_HEAD_DIM, RET_HEAD_DIM), F32),
        ],
        compiler_params=pltpu.CompilerParams(
            dimension_semantics=("arbitrary", "arbitrary"),
            vmem_limit_bytes=VMEM_LIMIT),
        name="retention",
    )(dec, proj, proj, proj, proj, gain, w_out)


def _post_norms(y, h_rows, gpost, gnext, valid):
    hn = h_rows + _rms(y, gpost)
    un = _rms(hn, gnext)
    if valid is not None:
        un = jnp.where(valid, un, 0.0)
    return hn, un.astype(BF16)


def _outproj_kernel(att_ref, ret_ref, w_ref, gpost_ref, gffn_ref, h_ref, ho_ref, u_ref):
    n_chunks = TM_OUT // OUT_CHUNK
    rows = lambda k: slice(k * OUT_CHUNK, (k + 1) * OUT_CHUNK)

    def project(k):
        r = rows(k)
        return (jnp.dot(att_ref[r, :], w_ref[0:ATT_WIDTH, :], preferred_element_type=F32)
                + jnp.dot(ret_ref[r, :], w_ref[ATT_WIDTH:, :], preferred_element_type=F32))

    y = project(0)
    for k in range(n_chunks):
        y_next = project(k + 1) if k + 1 < n_chunks else None
        hn, un = _post_norms(y, h_ref[rows(k), :], gpost_ref[...], gffn_ref[...], None)
        ho_ref[rows(k), :] = hn
        u_ref[rows(k), :] = un
        y = y_next


def _outproj(att, ret, w, gpost, gffn, h):
    row = lambda i: (i, 0)
    const = lambda i: (0, 0)
    return pl.pallas_call(
        _outproj_kernel,
        grid=(M_ROWS // TM_OUT,),
        in_specs=[
            pl.BlockSpec((TM_OUT, ATT_WIDTH), row),
            pl.BlockSpec((TM_OUT, RET_WIDTH), row),
            pl.BlockSpec((D_MODEL, D_MODEL), const),
            pl.BlockSpec((1, D_MODEL), const),
            pl.BlockSpec((1, D_MODEL), const),
            pl.BlockSpec((TM_OUT, D_MODEL), row),
        ],
        out_specs=[
            pl.BlockSpec((TM_OUT, D_MODEL), row),
            pl.BlockSpec((TM_OUT, D_MODEL), row),
        ],
        out_shape=[
            jax.ShapeDtypeStruct((M_ROWS, D_MODEL), F32),
            jax.ShapeDtypeStruct((M_ROWS, D_MODEL), BF16),
        ],
        input_output_aliases={5: 0},
        compiler_params=pltpu.CompilerParams(
            dimension_semantics=("arbitrary",),
            vmem_limit_bytes=VMEM_LIMIT),
        name="outproj",
    )(att, ret, w, gpost, gffn, h)


def _ffn_steps(u_ref, wg_ref, wu_ref, wd_ref, acc_ref, finish):
    j = pl.program_id(1)
    last = pl.num_programs(1) - 1

    def swiglu(rows):
        u = u_ref[rows, :]
        g = jnp.dot(u, wg_ref[...], preferred_element_type=F32)
        up = jnp.dot(u, wu_ref[...], preferred_element_type=F32)
        f = (g * (1.0 / (1.0 + jnp.exp(-g))) * up).astype(BF16)
        return jnp.dot(f, wd_ref[...], preferred_element_type=F32)

    @pl.when(j == 0)
    def _():
        acc_ref[...] = jnp.zeros((TM_FFN, D_MODEL), F32)

    @pl.when(j < last)
    def _():
        acc_ref[...] += swiglu(slice(None))

    @pl.when(j == last)
    def _():
        n_chunks = TM_FFN // FFN_CHUNK
        rows = lambda k: slice(k * FFN_CHUNK, (k + 1) * FFN_CHUNK)
        y_next = acc_ref[rows(0), :] + swiglu(rows(0))
        for k in range(n_chunks):
            y = y_next
            if k + 1 < n_chunks:
                y_next = acc_ref[rows(k + 1), :] + swiglu(rows(k + 1))
            finish(k, rows(k), y)


def _ffn_mid_kernel(layer_ref, u_ref, wg_ref, wu_ref, wd_ref, gpost_ref, gnext_ref, h_ref, win_ref,
                    ho_ref, un_ref, winb_ref):
    del layer_ref
    i = pl.program_id(0)
    winb_ref[...] = win_ref[...].astype(BF16)

    def finish(k, r, y):
        valid = _valid_rows(i * TM_FFN + k * FFN_CHUNK, FFN_CHUNK)
        hn, un = _post_norms(y, h_ref[r, :], gpost_ref[...], gnext_ref[...], valid)
        ho_ref[r, :] = hn
        un_ref[r, :] = un

    _ffn_steps(u_ref, wg_ref, wu_ref, wd_ref, ho_ref, finish)


def _ffn_mid(u, wg, wu, wd, gpost, gnext, h, w_in, next_layer):
    row = lambda i, j, l: (i, 0)
    const = lambda i, j, l: (0, 0)
    n_i = M_ROWS // TM_FFN
    n_j = D_FF // TF_FFN
    assert n_i >= IN_CAST_TILES and n_j * TN_IN == IN_COLS
    ct = D_MODEL // IN_CAST_TILES
    tile = lambda i: jnp.minimum(i, IN_CAST_TILES - 1)
    return pl.pallas_call(
        _ffn_mid_kernel,
        grid_spec=pltpu.PrefetchScalarGridSpec(
            num_scalar_prefetch=1,
            grid=(n_i, n_j),
            in_specs=[
                pl.BlockSpec((TM_FFN, D_MODEL), row),
                pl.BlockSpec((D_MODEL, TF_FFN), lambda i, j, l: (0, j)),
                pl.BlockSpec((D_MODEL, TF_FFN), lambda i, j, l: (0, j)),
                pl.BlockSpec((TF_FFN, D_MODEL), lambda i, j, l: (j, 0)),
                pl.BlockSpec((1, D_MODEL), const),
                pl.BlockSpec((1, D_MODEL), const),
                pl.BlockSpec((TM_FFN, D_MODEL), row),
                pl.BlockSpec((None, ct, TN_IN), lambda i, j, l: (l[0], tile(i), j)),
            ],
            out_specs=[
                pl.BlockSpec((TM_FFN, D_MODEL), row),
                pl.BlockSpec((TM_FFN, D_MODEL), row),
                pl.BlockSpec((ct, TN_IN), lambda i, j, l: (i, j)),
            ],
        ),
        out_shape=[
            jax.ShapeDtypeStruct((M_ROWS, D_MODEL), F32),
            jax.ShapeDtypeStruct((M_ROWS, D_MODEL), BF16),
            jax.ShapeDtypeStruct((n_i * ct, IN_COLS), BF16),
        ],
        input_output_aliases={7: 0},
        compiler_params=pltpu.CompilerParams(
            dimension_semantics=("arbitrary", "arbitrary"),
            vmem_limit_bytes=VMEM_LIMIT),
        name="ffn",
    )(next_layer, u, wg, wu, wd, gpost, gnext, h, w_in)


def _token_block_copies(tile, acc_ref, slot, out_ref, sem):
    copies = []
    for q in range(TM_FFN // BLOCK):
        gb = tile * (TM_FFN // BLOCK) + q
        b = sum((gb >= k * NB).astype(jnp.int32) for k in range(1, BATCH))
        n = gb - b * NB
        dst = pl.multiple_of((b * (SEQ // BLOCK) + jnp.maximum(n - 1, 0)) * BLOCK, BLOCK)
        copy = pltpu.make_async_copy(
            acc_ref.at[slot, pl.ds(q * BLOCK, BLOCK), :], out_ref.at[pl.ds(dst, BLOCK), :],
            sem.at[slot])
        copies.append((n >= 1, copy))
    return copies


def _ffn_last_kernel(u_ref, wg_ref, wu_ref, wd_ref, gpost_ref, h_ref, out_ref, acc_ref, sem):
    i = pl.program_id(0)
    j = pl.program_id(1)
    slot = i % 2
    acc = acc_ref.at[slot]

    def finish(k, r, y):
        acc[r, :] = h_ref[r, :] + _rms(y, gpost_ref[...])

    _ffn_steps(u_ref, wg_ref, wu_ref, wd_ref, acc, finish)

    @pl.when(j == pl.num_programs(1) - 1)
    def _():
        for is_token, copy in _token_block_copies(i, acc_ref, slot, out_ref, sem):
            @pl.when(is_token)
            def _():
                copy.start()

        @pl.when(i > 0)
        def _():
            for is_token, copy in _token_block_copies(i - 1, acc_ref, 1 - slot, out_ref, sem):
                @pl.when(is_token)
                def _():
                    copy.wait()

        @pl.when(i == pl.num_programs(0) - 1)
        def _():
            for is_token, copy in _token_block_copies(i, acc_ref, slot, out_ref, sem):
                @pl.when(is_token)
                def _():
                    copy.wait()


def _ffn_last(u, wg, wu, wd, gpost, h):
    row = lambda i, j: (i, 0)
    const = lambda i, j: (0, 0)
    return pl.pallas_call(
        _ffn_last_kernel,
        grid=(M_ROWS // TM_FFN, D_FF // TF_FFN),
        in_specs=[
            pl.BlockSpec((TM_FFN, D_MODEL), row),
            pl.BlockSpec((D_MODEL, TF_FFN), lambda i, j: (0, j)),
            pl.BlockSpec((D_MODEL, TF_FFN), lambda i, j: (0, j)),
            pl.BlockSpec((TF_FFN, D_MODEL), lambda i, j: (j, 0)),
            pl.BlockSpec((1, D_MODEL), const),
            pl.BlockSpec((TM_FFN, D_MODEL), row),
        ],
        out_specs=pl.BlockSpec(memory_space=pl.ANY),
        out_shape=jax.ShapeDtypeStruct((BATCH * SEQ, D_MODEL), F32),
        scratch_shapes=[
            pltpu.VMEM((2, TM_FFN, D_MODEL), F32),
            pltpu.SemaphoreType.DMA((2,)),
        ],
        compiler_params=pltpu.CompilerParams(
            dimension_semantics=("arbitrary", "arbitrary"),
            vmem_limit_bytes=VMEM_LIMIT),
        name="ffn_last",
    )(u, wg, wu, wd, gpost, h)


def _rope_tables():
    pos = (jnp.arange(NP) - PAD_FRONT).astype(F32)
    half_a = ROT_DIM // 2
    inv_a = ROPE_THETA ** (-jnp.arange(half_a, dtype=F32) / half_a)
    ang_a = pos[:, None] * inv_a[None, :]
    cos_a, sin_a = jnp.cos(ang_a), jnp.sin(ang_a)
    rest = ATT_HEAD_DIM - ROT_DIM
    ca = jnp.concatenate([cos_a, cos_a, jnp.ones((NP, rest), F32)], axis=1)
    s1 = jnp.concatenate([jnp.zeros((NP, half_a), F32), sin_a, jnp.zeros((NP, rest), F32)], axis=1)
    s2 = jnp.concatenate([-sin_a, jnp.zeros((NP, ATT_HEAD_DIM - half_a), F32)], axis=1)
    half_r = RET_HEAD_DIM // 2
    inv_r = RET_THETA ** (-jnp.arange(half_r, dtype=F32) / half_r)
    ang_r = pos[:, None] * inv_r[None, :]
    return ca, s1, s2, jnp.cos(ang_r), jnp.sin(ang_r)


def kernel(x, meta_tokens, w_in, w_out, attn_sink, ret_decay_fwd, ret_decay_bwd, ret_norm,
           norm_mix_pre, norm_mix_post, w_gate, w_up, w_down, norm_ffn_pre, norm_ffn_post):
    tabs = _rope_tables()
    row = lambda a, l: a[l].reshape(1, -1).astype(F32)
    w_in, w_out, w_gate, w_up, w_down = (
        w.astype(F32) for w in (w_in, w_out, w_gate, w_up, w_down))
    h, u, w_in_l = _embed(x, meta_tokens, row(norm_mix_pre, 0), w_in)
    for l in range(DEPTH):
        layer = jnp.full((1,), l, jnp.int32)
        proj, wg, wu, wd = _inproj(u, w_in_l, layer, tabs, w_gate, w_up, w_down)
        att = _attention(proj, attn_sink[l].astype(F32))
        dec = jnp.stack([ret_decay_fwd[l], ret_decay_bwd[l]]).astype(F32)
        ret, w_out_l = _retention(proj, dec, row(ret_norm, l), w_out, l)
        h, u = _outproj(att, ret, w_out_l, row(norm_mix_post, l), row(norm_ffn_pre, l), h)
        if l + 1 < DEPTH:
            h, u, w_in_l = _ffn_mid(u, wg, wu, wd, row(norm_ffn_post, l), row(norm_mix_pre, l + 1),
                                    h, w_in, jnp.full((1,), l + 1, jnp.int32))
        else:
            out = _ffn_last(u, wg, wu, wd, row(norm_ffn_post, l), h)
    return out.reshape(BATCH, SEQ, D_MODEL)
```

```python
import functools

import jax
import jax.numpy as jnp
from jax import lax
from jax.experimental import pallas as pl
from jax.experimental.pallas import tpu as pltpu

F32 = jnp.float32
BF16 = jnp.bfloat16

D_MODEL = 2048
BATCH = 4
SEQ = 4096
DEPTH = 4
N_META = 16
BLOCK = 128
WINDOW = 128
PAD_FRONT = BLOCK - N_META
NP = PAD_FRONT + N_META + SEQ
NB = NP // BLOCK
M_ROWS = BATCH * NP
ATT_HEAD_DIM = 128
ATT_WIDTH = 1024
ATT_HEADS = 8
ATT_KV_HEADS = 2
ATT_GROUP = ATT_HEADS // ATT_KV_HEADS
KV_WIDTH = ATT_KV_HEADS * ATT_HEAD_DIM
ROT_DIM = 32
ROPE_THETA = 500000.0
RET_WIDTH = 1024
RET_HEAD_DIM = 256
RET_HEADS = 4
RET_THETA = 10000.0
D_FF = 5632
IN_COLS = ATT_WIDTH + 2 * KV_WIDTH + 4 * RET_WIDTH
EPS = 1e-6
NEG = -1e30
LOG2E = 1.4426950408889634
ATT_QSCALE = ATT_HEAD_DIM ** -0.5 * LOG2E

COL_AQ = 0
COL_AK = ATT_WIDTH
COL_AV = COL_AK + KV_WIDTH
COL_RQ = COL_AV + KV_WIDTH
COL_RK = COL_RQ + RET_WIDTH
COL_RV = COL_RK + RET_WIDTH
COL_RG = COL_RV + RET_WIDTH

VMEM_LIMIT = 56 * 1024 * 1024

TM_IN = 2816
TN_IN = 512
TM_OUT = 768
TM_FFN = 768
TF_FFN = 512
IN_CAST_TILES = 16
CAST_TILES = 4
EMBED_BLOCKS = 3
RET_CHUNK = 256
RET_CHUNKS = (NP - BLOCK) // RET_CHUNK
IN_CHUNK = 176
OUT_CHUNK = 192
FFN_CHUNK = 256


def _rms(x, g):
    return x * lax.rsqrt(jnp.mean(x * x, axis=-1, keepdims=True) + EPS) * g


def _valid_rows(row0, rows):
    r = row0 + lax.broadcasted_iota(jnp.int32, (rows, 1), 0)
    ok = r >= 0
    for b in range(BATCH):
        ok = ok & ~((r >= b * NP) & (r < b * NP + PAD_FRONT))
    return ok


def _embed_kernel(*refs):
    x_refs = refs[:EMBED_BLOCKS]
    meta_ref, g_ref, w_ref, h_ref, u_ref, wb_ref = refs[EMBED_BLOCKS:]
    n = pl.program_id(1)
    wb_ref[...] = w_ref[...].astype(BF16)

    def copy(x_ref, rows):
        xv = x_ref[0]
        h_ref[rows, :] = xv
        u_ref[rows, :] = _rms(xv, g_ref[...]).astype(BF16)

    @pl.when(n == 0)
    def _():
        h_ref[0:PAD_FRONT, :] = jnp.zeros((PAD_FRONT, D_MODEL), F32)
        u_ref[0:PAD_FRONT, :] = jnp.zeros((PAD_FRONT, D_MODEL), BF16)
        m = meta_ref[...]
        h_ref[PAD_FRONT:BLOCK, :] = m
        u_ref[PAD_FRONT:BLOCK, :] = _rms(m, g_ref[...]).astype(BF16)

    @pl.when(n > 0)
    def _():
        copy(x_refs[0], slice(0, BLOCK))

    for t in range(1, EMBED_BLOCKS):
        copy(x_refs[t], slice(t * BLOCK, (t + 1) * BLOCK))


def _embed(x, meta, g, w_in):
    rows = EMBED_BLOCKS * BLOCK
    steps = NB // EMBED_BLOCKS
    wr, wc = D_MODEL // BATCH, IN_COLS // steps
    assert wr * BATCH == D_MODEL and wc * steps == IN_COLS

    def x_spec(t):
        return pl.BlockSpec(
            (1, BLOCK, D_MODEL), lambda b, n: (b, jnp.maximum(n * EMBED_BLOCKS + t - 1, 0), 0))

    return pl.pallas_call(
        _embed_kernel,
        grid=(BATCH, steps),
        in_specs=[x_spec(t) for t in range(EMBED_BLOCKS)] + [
            pl.BlockSpec((N_META, D_MODEL), lambda b, n: (0, 0)),
            pl.BlockSpec((1, D_MODEL), lambda b, n: (0, 0)),
            pl.BlockSpec((None, wr, wc), lambda b, n: (0, b, n)),
        ],
        out_specs=[
            pl.BlockSpec((rows, D_MODEL), lambda b, n: (b * steps + n, 0)),
            pl.BlockSpec((rows, D_MODEL), lambda b, n: (b * steps + n, 0)),
            pl.BlockSpec((wr, wc), lambda b, n: (b, n)),
        ],
        out_shape=[
            jax.ShapeDtypeStruct((M_ROWS, D_MODEL), F32),
            jax.ShapeDtypeStruct((M_ROWS, D_MODEL), BF16),
            jax.ShapeDtypeStruct((D_MODEL, IN_COLS), BF16),
        ],
        compiler_params=pltpu.CompilerParams(
            dimension_semantics=("arbitrary", "arbitrary"),
            vmem_limit_bytes=VMEM_LIMIT),
        name="embed",
    )(*([x] * EMBED_BLOCKS), meta, g, w_in)


def _inproj_kernel(layer_ref, u_ref, w_ref, ca_ref, s1_ref, s2_ref, cr_ref, sr_ref, gain_ref,
                   wg_in, wu_in, wd_in, o_ref, wg_out, wu_out, wd_out):
    del layer_ref
    j = pl.program_id(1)
    tile_row0 = (pl.program_id(0) * TM_IN) % NP

    def tab_rows(r):
        start = tile_row0 + r.start
        start = jnp.where(start >= NP, start - NP, start)
        return pl.ds(pl.multiple_of(start, 8), r.stop - r.start)

    def rope_att(x, r):
        tr = tab_rows(r)
        return (x * ca_ref[tr, :]
                + pltpu.roll(x, ROT_DIM // 2, 1) * s1_ref[tr, :]
                + pltpu.roll(x, ATT_HEAD_DIM - ROT_DIM // 2, 1) * s2_ref[tr, :])

    def epi_att(n_heads, scale):
        def epi(acc, r):
            for hh in range(n_heads):
                sl = slice(hh * ATT_HEAD_DIM, (hh + 1) * ATT_HEAD_DIM)
                y = rope_att(acc[:, sl], r)
                if scale != 1.0:
                    y = y * scale
                o_ref[r, sl] = y.astype(BF16)
            rest = n_heads * ATT_HEAD_DIM
            if rest < TN_IN:
                o_ref[r, rest:] = acc[:, rest:].astype(BF16)
        return epi

    def epi_ret(scale):
        def epi(acc, r):
            c = cr_ref[tab_rows(r), :]
            s = sr_ref[tab_rows(r), :]
            for hh in range(TN_IN // RET_HEAD_DIM):
                base = hh * RET_HEAD_DIM
                x1 = acc[:, base:base + 128]
                x2 = acc[:, base + 128:base + 256]
                y1 = x1 * c - x2 * s
                y2 = x2 * c + x1 * s
                if scale != 1.0:
                    y1 = y1 * scale
                    y2 = y2 * scale
                o_ref[r, base:base + 128] = y1.astype(BF16)
                o_ref[r, base + 128:base + 256] = y2.astype(BF16)
        return epi

    def epi_plain(acc, r):
        o_ref[r, :] = acc.astype(BF16)

    def epi_gate(acc, r):
        o_ref[r, :] = (acc * (1.0 / (1.0 + jnp.exp(-acc))) * gain_ref[...]).astype(BF16)

    def run(epi):
        wg_out[...] = wg_in[...].astype(BF16)
        wu_out[...] = wu_in[...].astype(BF16)
        wd_out[...] = wd_in[...].astype(BF16)
        for k in range(TM_IN // IN_CHUNK):
            r = slice(k * IN_CHUNK, (k + 1) * IN_CHUNK)
            epi(jnp.dot(u_ref[r, :], w_ref[...], preferred_element_type=F32), r)

    n_aq = ATT_WIDTH // TN_IN
    j_akv = n_aq
    j_rq = COL_RQ // TN_IN
    j_rk = COL_RK // TN_IN
    j_rv = COL_RV // TN_IN
    j_rg = COL_RG // TN_IN

    @pl.when(j < n_aq)
    def _():
        run(epi_att(TN_IN // ATT_HEAD_DIM, ATT_QSCALE))

    @pl.when(j == j_akv)
    def _():
        run(epi_att(ATT_KV_HEADS, 1.0))

    @pl.when((j >= j_rq) & (j < j_rk))
    def _():
        run(epi_ret(1.0))

    @pl.when((j >= j_rk) & (j < j_rv))
    def _():
        run(epi_ret(RET_HEAD_DIM ** -0.5))

    @pl.when((j >= j_rv) & (j < j_rg))
    def _():
        run(epi_plain)

    @pl.when(j >= j_rg)
    def _():
        run(epi_gate)


def _inproj(u, w, layer, tabs, gain, w_gate, w_up, w_down):
    tab_spec = pl.BlockSpec((NP, 128), lambda i, j, l: (0, 0))
    n_i = M_ROWS // TM_IN
    n_j = IN_COLS // TN_IN
    assert n_i >= CAST_TILES and n_j * TF_FFN == D_FF
    ct = D_MODEL // CAST_TILES
    tile = lambda i: jnp.minimum(i, CAST_TILES - 1)
    return pl.pallas_call(
        _inproj_kernel,
        grid_spec=pltpu.PrefetchScalarGridSpec(
            num_scalar_prefetch=1,
            grid=(n_i, n_j),
            in_specs=[
                pl.BlockSpec((TM_IN, D_MODEL), lambda i, j, l: (i, 0)),
                pl.BlockSpec((D_MODEL, TN_IN), lambda i, j, l: (0, j)),
                tab_spec, tab_spec, tab_spec, tab_spec, tab_spec,
                pl.BlockSpec((1, TN_IN), lambda i, j, l: (0, jnp.maximum(j - COL_RG // TN_IN, 0))),
                pl.BlockSpec((None, ct, TF_FFN), lambda i, j, l: (l[0], tile(i), j)),
                pl.BlockSpec((None, ct, TF_FFN), lambda i, j, l: (l[0], tile(i), j)),
                pl.BlockSpec((None, TF_FFN, ct), lambda i, j, l: (l[0], j, tile(i))),
            ],
            out_specs=[
                pl.BlockSpec((TM_IN, TN_IN), lambda i, j, l: (i, j)),
                pl.BlockSpec((ct, TF_FFN), lambda i, j, l: (i, j)),
                pl.BlockSpec((ct, TF_FFN), lambda i, j, l: (i, j)),
                pl.BlockSpec((TF_FFN, ct), lambda i, j, l: (j, i)),
            ],
        ),
        out_shape=[
            jax.ShapeDtypeStruct((M_ROWS, IN_COLS), BF16),
            jax.ShapeDtypeStruct((n_i * ct, D_FF), BF16),
            jax.ShapeDtypeStruct((n_i * ct, D_FF), BF16),
            jax.ShapeDtypeStruct((D_FF, n_i * ct), BF16),
        ],
        compiler_params=pltpu.CompilerParams(
            dimension_semantics=("arbitrary", "arbitrary"),
            vmem_limit_bytes=VMEM_LIMIT),
        name="inproj",
    )(layer, u, w, *tabs, gain, w_gate, w_up, w_down)


def _att_mask(n):
    row = lax.broadcasted_iota(jnp.int32, (BLOCK, 4 * BLOCK), 0)
    col = lax.broadcasted_iota(jnp.int32, (BLOCK, 4 * BLOCK), 1)
    qi = n * BLOCK + row
    kj = (n - 1) * BLOCK + col
    is_band = col < 3 * BLOCK
    band = is_band & (jnp.abs(qi - kj) <= WINDOW) & (kj >= PAD_FRONT) & (kj < NP)
    mj = col - 3 * BLOCK
    meta = (~is_band) & (mj >= PAD_FRONT) & (jnp.abs(qi - mj) > WINDOW)
    return band | meta


def _attn_kernel(sink_ref, q_ref, k_ref, v_ref, o_ref, bias_ref):
    kv = pl.program_id(1)
    rows = ATT_GROUP * BLOCK

    head = lax.broadcasted_iota(jnp.int32, (rows, 1), 0) // BLOCK
    sink = jnp.zeros((rows, 1), F32)
    for g in range(ATT_GROUP):
        sink = jnp.where(head == g, sink_ref[kv * ATT_GROUP + g] * LOG2E, sink)

    k_meta = k_ref[0:BLOCK, :]
    v_meta = v_ref[0:BLOCK, :]
    ones = jnp.ones((4 * BLOCK, ATT_HEAD_DIM), BF16)

    bias_ref[...] = jnp.where(_att_mask(2), 0.0, NEG).astype(F32)

    def block(n, bias, own_bias):
        r_own = pl.ds(pl.multiple_of(n * BLOCK, BLOCK), BLOCK)
        r_prev = pl.ds(pl.multiple_of(jnp.maximum(n - 1, 0) * BLOCK, BLOCK), BLOCK)
        r_next = pl.ds(pl.multiple_of(jnp.minimum(n + 1, NB - 1) * BLOCK, BLOCK), BLOCK)
        q = q_ref[r_own, :]
        qs = jnp.concatenate(
            [q[:, g * ATT_HEAD_DIM:(g + 1) * ATT_HEAD_DIM] for g in range(ATT_GROUP)], axis=0)
        ks = jnp.concatenate([k_ref[r_prev, :], k_ref[r_own, :], k_ref[r_next, :], k_meta], axis=0)
        vs = jnp.concatenate([v_ref[r_prev, :], v_ref[r_own, :], v_ref[r_next, :], v_meta], axis=0)
        vs = jnp.concatenate([vs, ones], axis=1)
        s = lax.dot_general(qs, ks, (((1,), (1,)), ((), ())), preferred_element_type=F32)
        bias4 = jnp.concatenate([bias] * ATT_GROUP, axis=0)
        if own_bias:
            s = s + bias4
        else:
            s = jnp.concatenate([s[:, :BLOCK] + bias4[:, :BLOCK], s[:, BLOCK:2 * BLOCK],
                                 s[:, 2 * BLOCK:] + bias4[:, 2 * BLOCK:]], axis=1)
        m = jnp.maximum(jnp.max(s, axis=-1, keepdims=True), sink)
        p = jnp.exp2(s - m).astype(BF16)
        o = jnp.dot(p, vs, preferred_element_type=F32)
        o = o[:, :ATT_HEAD_DIM] / (o[:, ATT_HEAD_DIM:] + jnp.exp2(sink - m))
        for g in range(ATT_GROUP):
            o_ref[r_own, g * ATT_HEAD_DIM:(g + 1) * ATT_HEAD_DIM] = (
                o[g * BLOCK:(g + 1) * BLOCK, :].astype(BF16))

    def edge(n):
        block(n, jnp.where(_att_mask(n), 0.0, NEG).astype(F32), True)

    edge(0)
    edge(1)

    def body(n, carry):
        block(n, bias_ref[...], False)
        return carry

    lax.fori_loop(2, NB - 1, body, 0, unroll=10)
    edge(NB - 1)


def _attention(proj, sink):
    q_blk = ATT_GROUP * ATT_HEAD_DIM
    return pl.pallas_call(
        _attn_kernel,
        grid=(BATCH, ATT_KV_HEADS),
        in_specs=[
            pl.BlockSpec(memory_space=pltpu.SMEM),
            pl.BlockSpec((NP, q_blk), lambda b, k: (b, k)),
            pl.BlockSpec((NP, ATT_HEAD_DIM), lambda b, k: (b, COL_AK // ATT_HEAD_DIM + k)),
            pl.BlockSpec((NP, ATT_HEAD_DIM), lambda b, k: (b, COL_AV // ATT_HEAD_DIM + k)),
        ],
        out_specs=pl.BlockSpec((NP, q_blk), lambda b, k: (b, k)),
        out_shape=jax.ShapeDtypeStruct((M_ROWS, ATT_WIDTH), BF16),
        scratch_shapes=[pltpu.VMEM((BLOCK, 4 * BLOCK), F32)],
        compiler_params=pltpu.CompilerParams(
            dimension_semantics=("arbitrary", "arbitrary"),
            vmem_limit_bytes=VMEM_LIMIT),
        name="attention",
    )(sink, proj, proj, proj)


def _ret_kernel(dec_ref, q_ref, k_ref, v_ref, g_ref, wo_ref,
                o_ref, wob_ref, st_ref, sf_ref, sb_ref):
    wob_ref[...] = wo_ref[...].astype(BF16)
    hd = pl.program_id(1)
    lgf = -jnp.exp(jnp.full((1, RET_HEAD_DIM), dec_ref[0, hd], F32))
    lgb = -jnp.exp(jnp.full((1, RET_HEAD_DIM), dec_ref[1, hd], F32))

    def weights(L):
        idx = lax.broadcasted_iota(jnp.int32, (L, RET_HEAD_DIM), 0).astype(F32)
        r = lax.broadcasted_iota(jnp.int32, (L, L), 0)
        c = lax.broadcasted_iota(jnp.int32, (L, L), 1)
        diff = (r - c).astype(F32)
        return dict(
            xi_f=jnp.exp(lgf * (idx + 1.0)).astype(BF16),
            zeta_f=jnp.exp(lgf * (L - 1.0 - idx)).astype(BF16),
            xi_b=jnp.exp(lgb * (L - idx)).astype(BF16),
            zeta_b=jnp.exp(lgb * idx).astype(BF16),
            dec_f=jnp.exp(lgf * float(L)),
            dec_b=jnp.exp(lgb * float(L)),
            dmask=jnp.where(diff >= 0.0,
                            jnp.exp(lgf[:, :L] * jnp.maximum(diff, 0.0)),
                            jnp.exp(lgb[:, :L] * jnp.maximum(-diff, 0.0))))

    w_head = weights(BLOCK)
    w_main = weights(RET_CHUNK)

    def main_rows(c):
        return pl.ds(pl.multiple_of(BLOCK + (c - 1) * RET_CHUNK, BLOCK), RET_CHUNK)

    head_rows = slice(0, BLOCK)

    def kv_outer(rows, zeta):
        return lax.dot_general(k_ref[rows, :] * zeta, v_ref[rows, :], (((0,), (0,)), ((), ())),
                               preferred_element_type=F32)

    zeros = jnp.zeros((RET_HEAD_DIM, RET_HEAD_DIM), F32)
    st_ref[0, 0:RET_HEAD_DIM, :] = zeros.astype(BF16)
    sf_ref[...] = kv_outer(head_rows, w_head["zeta_f"])
    sb_ref[...] = zeros

    def scan(t, carry):
        cf = 1 + t
        cb = RET_CHUNKS - t
        st_ref[cf, 0:RET_HEAD_DIM, :] = sf_ref[...].astype(BF16)
        st_ref[cb, RET_HEAD_DIM:, :] = sb_ref[...].astype(BF16)
        sf_ref[...] = sf_ref[...] * w_main["dec_f"] + kv_outer(main_rows(cf), w_main["zeta_f"])
        sb_ref[...] = sb_ref[...] * w_main["dec_b"] + kv_outer(main_rows(cb), w_main["zeta_b"])
        return carry

    lax.fori_loop(0, RET_CHUNKS, scan, 0, unroll=2)
    st_ref[0, RET_HEAD_DIM:, :] = sb_ref[...].astype(BF16)

    def out(rows, c, w):
        q = q_ref[rows, :]
        k = k_ref[rows, :]
        v = v_ref[rows, :]
        a = lax.dot_general(q, k, (((1,), (1,)), ((), ())), preferred_element_type=F32)
        p = (a * w["dmask"]).astype(BF16)
        qc = jnp.concatenate([q * w["xi_f"], q * w["xi_b"]], axis=1)
        y = (jnp.dot(p, v, preferred_element_type=F32)
             + jnp.dot(qc, st_ref[c], preferred_element_type=F32))
        y = y * lax.rsqrt(jnp.mean(y * y, axis=-1, keepdims=True) + EPS)
        o_ref[rows, :] = (y * g_ref[rows, :].astype(F32)).astype(BF16)

    out(head_rows, 0, w_head)

    def out_main(t, carry):
        c = 1 + t
        out(main_rows(c), c, w_main)
        return carry

    lax.fori_loop(0, RET_CHUNKS, out_main, 0, unroll=16)


def _retention(proj, dec, w_out, layer):
    def col_spec(col0):
        return pl.BlockSpec((NP, RET_HEAD_DIM), lambda b, h: (b, col0 // RET_HEAD_DIM + h))

    wr = D_MODEL // (BATCH * RET_HEADS)
    return pl.pallas_call(
        _ret_kernel,
        grid=(BATCH, RET_HEADS),
        in_specs=[
            pl.BlockSpec(memory_space=pltpu.SMEM),
            col_spec(COL_RQ), col_spec(COL_RK), col_spec(COL_RV), col_spec(COL_RG),
            pl.BlockSpec((None, wr, D_MODEL), lambda b, h: (layer, b * RET_HEADS + h, 0)),
        ],
        out_specs=[
            pl.BlockSpec((NP, RET_HEAD_DIM), lambda b, h: (b, h)),
            pl.BlockSpec((wr, D_MODEL), lambda b, h: (b * RET_HEADS + h, 0)),
        ],
        out_shape=[
            jax.ShapeDtypeStruct((M_ROWS, RET_WIDTH), BF16),
            jax.ShapeDtypeStruct((D_MODEL, D_MODEL), BF16),
        ],
        scratch_shapes=[
            pltpu.VMEM((RET_CHUNKS + 1, 2 * RET_HEAD_DIM, RET_HEAD_DIM), BF16),
            pltpu.VMEM((RET_HEAD_DIM, RET_HEAD_DIM), F32),
            pltpu.VMEM((RET_HEAD_DIM, RET_HEAD_DIM), F32),
        ],
        compiler_params=pltpu.CompilerParams(
            dimension_semantics=("arbitrary", "arbitrary"),
            vmem_limit_bytes=VMEM_LIMIT),
        name="retention",
    )(dec, proj, proj, proj, proj, w_out)


def _post_norms(y, h_rows, gpost, gnext, valid):
    hn = h_rows + _rms(y, gpost)
    un = _rms(hn, gnext)
    if valid is not None:
        un = jnp.where(valid, un, 0.0)
    return hn, un.astype(BF16)


def _outproj_kernel(att_ref, ret_ref, w_ref, gpost_ref, gffn_ref, h_ref, ho_ref, u_ref):
    n_chunks = TM_OUT // OUT_CHUNK
    rows = lambda k: slice(k * OUT_CHUNK, (k + 1) * OUT_CHUNK)

    def project(k):
        r = rows(k)
        return (jnp.dot(att_ref[r, :], w_ref[0:ATT_WIDTH, :], preferred_element_type=F32)
                + jnp.dot(ret_ref[r, :], w_ref[ATT_WIDTH:, :], preferred_element_type=F32))

    y = project(0)
    for k in range(n_chunks):
        y_next = project(k + 1) if k + 1 < n_chunks else None
        hn, un = _post_norms(y, h_ref[rows(k), :], gpost_ref[...], gffn_ref[...], None)
        ho_ref[rows(k), :] = hn
        u_ref[rows(k), :] = un
        y = y_next


def _outproj(att, ret, w, gpost, gffn, h):
    row = lambda i: (i, 0)
    const = lambda i: (0, 0)
    return pl.pallas_call(
        _outproj_kernel,
        grid=(M_ROWS // TM_OUT,),
        in_specs=[
            pl.BlockSpec((TM_OUT, ATT_WIDTH), row),
            pl.BlockSpec((TM_OUT, RET_WIDTH), row),
            pl.BlockSpec((D_MODEL, D_MODEL), const),
            pl.BlockSpec((1, D_MODEL), const),
            pl.BlockSpec((1, D_MODEL), const),
            pl.BlockSpec((TM_OUT, D_MODEL), row),
        ],
        out_specs=[
            pl.BlockSpec((TM_OUT, D_MODEL), row),
            pl.BlockSpec((TM_OUT, D_MODEL), row),
        ],
        out_shape=[
            jax.ShapeDtypeStruct((M_ROWS, D_MODEL), F32),
            jax.ShapeDtypeStruct((M_ROWS, D_MODEL), BF16),
        ],
        input_output_aliases={5: 0},
        compiler_params=pltpu.CompilerParams(
            dimension_semantics=("arbitrary",),
            vmem_limit_bytes=VMEM_LIMIT),
        name="outproj",
    )(att, ret, w, gpost, gffn, h)


def _ffn_steps(u_ref, wg_ref, wu_ref, wd_ref, acc_ref, finish, side_job=lambda: None):
    j = pl.program_id(1)
    last = pl.num_programs(1) - 1

    def swiglu(rows):
        u = u_ref[rows, :]
        g = jnp.dot(u, wg_ref[...], preferred_element_type=F32)
        up = jnp.dot(u, wu_ref[...], preferred_element_type=F32)
        f = (g * (1.0 / (1.0 + jnp.exp(-g))) * up).astype(BF16)
        return jnp.dot(f, wd_ref[...], preferred_element_type=F32)

    @pl.when(j == 0)
    def _():
        acc_ref[...] = jnp.zeros((TM_FFN, D_MODEL), F32)

    @pl.when(j < last)
    def _():
        side_job()
        acc_ref[...] += swiglu(slice(None))

    @pl.when(j == last)
    def _():
        side_job()
        n_chunks = TM_FFN // FFN_CHUNK
        rows = lambda k: slice(k * FFN_CHUNK, (k + 1) * FFN_CHUNK)
        y_next = acc_ref[rows(0), :] + swiglu(rows(0))
        for k in range(n_chunks):
            y = y_next
            if k + 1 < n_chunks:
                y_next = acc_ref[rows(k + 1), :] + swiglu(rows(k + 1))
            finish(k, rows(k), y)


def _ffn_mid_kernel(layer_ref, u_ref, wg_ref, wu_ref, wd_ref, gpost_ref, gnext_ref, h_ref, win_ref,
                    ho_ref, un_ref, winb_ref):
    del layer_ref
    i = pl.program_id(0)

    def cast_next_w_in_tile():
        winb_ref[...] = win_ref[...].astype(BF16)

    def finish(k, r, y):
        valid = _valid_rows(i * TM_FFN + k * FFN_CHUNK, FFN_CHUNK)
        hn, un = _post_norms(y, h_ref[r, :], gpost_ref[...], gnext_ref[...], valid)
        ho_ref[r, :] = hn
        un_ref[r, :] = un

    _ffn_steps(u_ref, wg_ref, wu_ref, wd_ref, ho_ref, finish, cast_next_w_in_tile)


def _ffn_mid(u, wg, wu, wd, gpost, gnext, h, w_in, next_layer):
    row = lambda i, j, l: (i, 0)
    const = lambda i, j, l: (0, 0)
    n_i = M_ROWS // TM_FFN
    n_j = D_FF // TF_FFN
    assert n_i >= IN_CAST_TILES and n_j * TN_IN == IN_COLS
    ct = D_MODEL // IN_CAST_TILES
    tile = lambda i: jnp.minimum(i, IN_CAST_TILES - 1)
    return pl.pallas_call(
        _ffn_mid_kernel,
        grid_spec=pltpu.PrefetchScalarGridSpec(
            num_scalar_prefetch=1,
            grid=(n_i, n_j),
            in_specs=[
                pl.BlockSpec((TM_FFN, D_MODEL), row),
                pl.BlockSpec((D_MODEL, TF_FFN), lambda i, j, l: (0, j)),
                pl.BlockSpec((D_MODEL, TF_FFN), lambda i, j, l: (0, j)),
                pl.BlockSpec((TF_FFN, D_MODEL), lambda i, j, l: (j, 0)),
                pl.BlockSpec((1, D_MODEL), const),
                pl.BlockSpec((1, D_MODEL), const),
                pl.BlockSpec((TM_FFN, D_MODEL), row),
                pl.BlockSpec((None, ct, TN_IN), lambda i, j, l: (l[0], tile(i), j)),
            ],
            out_specs=[
                pl.BlockSpec((TM_FFN, D_MODEL), row),
                pl.BlockSpec((TM_FFN, D_MODEL), row),
                pl.BlockSpec((ct, TN_IN), lambda i, j, l: (i, j)),
            ],
        ),
        out_shape=[
            jax.ShapeDtypeStruct((M_ROWS, D_MODEL), F32),
            jax.ShapeDtypeStruct((M_ROWS, D_MODEL), BF16),
            jax.ShapeDtypeStruct((n_i * ct, IN_COLS), BF16),
        ],
        input_output_aliases={7: 0},
        compiler_params=pltpu.CompilerParams(
            dimension_semantics=("arbitrary", "arbitrary"),
            vmem_limit_bytes=VMEM_LIMIT),
        name="ffn",
    )(next_layer, u, wg, wu, wd, gpost, gnext, h, w_in)


def _token_block_copies(tile, acc_ref, slot, out_ref, sem):
    copies = []
    for q in range(TM_FFN // BLOCK):
        gb = tile * (TM_FFN // BLOCK) + q
        b = sum((gb >= k * NB).astype(jnp.int32) for k in range(1, BATCH))
        n = gb - b * NB
        dst = pl.multiple_of((b * (SEQ // BLOCK) + jnp.maximum(n - 1, 0)) * BLOCK, BLOCK)
        copy = pltpu.make_async_copy(
            acc_ref.at[slot, pl.ds(q * BLOCK, BLOCK), :], out_ref.at[pl.ds(dst, BLOCK), :],
            sem.at[slot])
        copies.append((n >= 1, copy))
    return copies


def _ffn_last_kernel(u_ref, wg_ref, wu_ref, wd_ref, gpost_ref, h_ref, out_ref, acc_ref, sem):
    i = pl.program_id(0)
    j = pl.program_id(1)
    slot = i % 2
    acc = acc_ref.at[slot]

    def finish(k, r, y):
        acc[r, :] = h_ref[r, :] + _rms(y, gpost_ref[...])

    _ffn_steps(u_ref, wg_ref, wu_ref, wd_ref, acc, finish)

    @pl.when(j == pl.num_programs(1) - 1)
    def _():
        for is_token, copy in _token_block_copies(i, acc_ref, slot, out_ref, sem):
            @pl.when(is_token)
            def _():
                copy.start()

        @pl.when(i > 0)
        def _():
            for is_token, copy in _token_block_copies(i - 1, acc_ref, 1 - slot, out_ref, sem):
                @pl.when(is_token)
                def _():
                    copy.wait()

        @pl.when(i == pl.num_programs(0) - 1)
        def _():
            for is_token, copy in _token_block_copies(i, acc_ref, slot, out_ref, sem):
                @pl.when(is_token)
                def _():
                    copy.wait()


def _ffn_last(u, wg, wu, wd, gpost, h):
    row = lambda i, j: (i, 0)
    const = lambda i, j: (0, 0)
    return pl.pallas_call(
        _ffn_last_kernel,
        grid=(M_ROWS // TM_FFN, D_FF // TF_FFN),
        in_specs=[
            pl.BlockSpec((TM_FFN, D_MODEL), row),
            pl.BlockSpec((D_MODEL, TF_FFN), lambda i, j: (0, j)),
            pl.BlockSpec((D_MODEL, TF_FFN), lambda i, j: (0, j)),
            pl.BlockSpec((TF_FFN, D_MODEL), lambda i, j: (j, 0)),
            pl.BlockSpec((1, D_MODEL), const),
            pl.BlockSpec((TM_FFN, D_MODEL), row),
        ],
        out_specs=pl.BlockSpec(memory_space=pl.ANY),
        out_shape=jax.ShapeDtypeStruct((BATCH * SEQ, D_MODEL), F32),
        scratch_shapes=[
            pltpu.VMEM((2, TM_FFN, D_MODEL), F32),
            pltpu.SemaphoreType.DMA((2,)),
        ],
        compiler_params=pltpu.CompilerParams(
            dimension_semantics=("arbitrary", "arbitrary"),
            vmem_limit_bytes=VMEM_LIMIT),
        name="ffn_last",
    )(u, wg, wu, wd, gpost, h)


def _rope_tables():
    pos = (jnp.arange(NP) - PAD_FRONT).astype(F32)
    half_a = ROT_DIM // 2
    inv_a = ROPE_THETA ** (-jnp.arange(half_a, dtype=F32) / half_a)
    ang_a = pos[:, None] * inv_a[None, :]
    cos_a, sin_a = jnp.cos(ang_a), jnp.sin(ang_a)
    rest = ATT_HEAD_DIM - ROT_DIM
    ca = jnp.concatenate([cos_a, cos_a, jnp.ones((NP, rest), F32)], axis=1)
    s1 = jnp.concatenate([jnp.zeros((NP, half_a), F32), sin_a, jnp.zeros((NP, rest), F32)], axis=1)
    s2 = jnp.concatenate([-sin_a, jnp.zeros((NP, ATT_HEAD_DIM - half_a), F32)], axis=1)
    half_r = RET_HEAD_DIM // 2
    inv_r = RET_THETA ** (-jnp.arange(half_r, dtype=F32) / half_r)
    ang_r = pos[:, None] * inv_r[None, :]
    return ca, s1, s2, jnp.cos(ang_r), jnp.sin(ang_r)


def kernel(x, meta_tokens, w_in, w_out, attn_sink, ret_decay_fwd, ret_decay_bwd, ret_norm,
           norm_mix_pre, norm_mix_post, w_gate, w_up, w_down, norm_ffn_pre, norm_ffn_post):
    tabs = _rope_tables()
    row = lambda a, l: a[l].reshape(1, -1).astype(F32)
    w_in, w_out, w_gate, w_up, w_down = (
        w.astype(F32) for w in (w_in, w_out, w_gate, w_up, w_down))
    h, u, w_in_l = _embed(x, meta_tokens, row(norm_mix_pre, 0), w_in)
    for l in range(DEPTH):
        layer = jnp.full((1,), l, jnp.int32)
        proj, wg, wu, wd = _inproj(u, w_in_l, layer, tabs, row(ret_norm, l), w_gate, w_up, w_down)
        att = _attention(proj, attn_sink[l].astype(F32))
        dec = jnp.stack([ret_decay_fwd[l], ret_decay_bwd[l]]).astype(F32)
        ret, w_out_l = _retention(proj, dec, w_out, l)
        h, u = _outproj(att, ret, w_out_l, row(norm_mix_post, l), row(norm_ffn_pre, l), h)
        if l + 1 < DEPTH:
            h, u, w_in_l = _ffn_mid(u, wg, wu, wd, row(norm_ffn_post, l), row(norm_mix_pre, l + 1),
                                    h, w_in, jnp.full((1,), l + 1, jnp.int32))
        else:
            out = _ffn_last(u, wg, wu, wd, row(norm_ffn_post, l), h)
    return out.reshape(BATCH, SEQ, D_MODEL)
```

```python
import functools

import jax
import jax.numpy as jnp
from jax import lax
from jax.experimental import pallas as pl
from jax.experimental.pallas import tpu as pltpu

F32 = jnp.float32
BF16 = jnp.bfloat16

D_MODEL = 2048
BATCH = 4
SEQ = 4096
DEPTH = 4
N_META = 16
BLOCK = 128
SLAB = 128
WINDOW = 128
PAD_FRONT = BLOCK - N_META
NP = PAD_FRONT + N_META + SEQ
NB = NP // BLOCK
M_ROWS = BATCH * NP
ATT_HEAD_DIM = 128
ATT_WIDTH = 1024
ATT_HEADS = 8
ATT_KV_HEADS = 2
ATT_GROUP = ATT_HEADS // ATT_KV_HEADS
KV_WIDTH = ATT_KV_HEADS * ATT_HEAD_DIM
ROT_DIM = 32
ROPE_THETA = 500000.0
RET_WIDTH = 1024
RET_HEAD_DIM = 256
RET_HEADS = 4
RET_THETA = 10000.0
D_FF = 5632
IN_COLS = ATT_WIDTH + 2 * KV_WIDTH + 4 * RET_WIDTH
EPS = 1e-6
NEG = -1e30
LOG2E = 1.4426950408889634
ATT_QSCALE = ATT_HEAD_DIM ** -0.5 * LOG2E

COL_AQ = 0
COL_AK = ATT_WIDTH
COL_AV = COL_AK + KV_WIDTH
COL_RQ = COL_AV + KV_WIDTH
COL_RK = COL_RQ + RET_WIDTH
COL_RV = COL_RK + RET_WIDTH
COL_RG = COL_RV + RET_WIDTH

VMEM_LIMIT = 56 * 1024 * 1024

TM_IN = 1408
TN_IN = 512
TM_OUT = 768
TM_FFN = 768
TF_FFN = 512
IN_CAST_TILES = 16
CAST_TILES = 8
EMBED_BLOCKS = 3
RET_CHUNK = 256
RET_CHUNKS = (NP - BLOCK) // RET_CHUNK
IN_CHUNK = 176
OUT_CHUNK = 192
FFN_CHUNK = 256


def _rms(x, g):
    return x * lax.rsqrt(jnp.mean(x * x, axis=-1, keepdims=True) + EPS) * g


def _valid_rows(row0, rows):
    r = row0 + lax.broadcasted_iota(jnp.int32, (rows, 1), 0)
    ok = r >= 0
    for b in range(BATCH):
        ok = ok & ~((r >= b * NP) & (r < b * NP + PAD_FRONT))
    return ok


def _embed_kernel(*refs):
    x_refs = refs[:EMBED_BLOCKS]
    meta_ref, g_ref, w_ref, h_ref, u_ref, wb_ref = refs[EMBED_BLOCKS:]
    n = pl.program_id(1)
    wb_ref[...] = w_ref[...].astype(BF16)

    def copy(x_ref, rows):
        xv = x_ref[0]
        h_ref[rows, :] = xv
        u_ref[rows, :] = _rms(xv, g_ref[...]).astype(BF16)

    @pl.when(n == 0)
    def _():
        h_ref[0:PAD_FRONT, :] = jnp.zeros((PAD_FRONT, D_MODEL), F32)
        u_ref[0:PAD_FRONT, :] = jnp.zeros((PAD_FRONT, D_MODEL), BF16)
        m = meta_ref[...]
        h_ref[PAD_FRONT:BLOCK, :] = m
        u_ref[PAD_FRONT:BLOCK, :] = _rms(m, g_ref[...]).astype(BF16)

    @pl.when(n > 0)
    def _():
        copy(x_refs[0], slice(0, BLOCK))

    for t in range(1, EMBED_BLOCKS):
        copy(x_refs[t], slice(t * BLOCK, (t + 1) * BLOCK))


def _embed(x, meta, g, w_in):
    rows = EMBED_BLOCKS * BLOCK
    steps = NB // EMBED_BLOCKS
    wr, wc = D_MODEL // BATCH, IN_COLS // steps
    assert wr * BATCH == D_MODEL and wc * steps == IN_COLS and wc == TN_IN

    def x_spec(t):
        return pl.BlockSpec(
            (1, BLOCK, D_MODEL), lambda b, n: (b, jnp.maximum(n * EMBED_BLOCKS + t - 1, 0), 0))

    return pl.pallas_call(
        _embed_kernel,
        grid=(BATCH, steps),
        in_specs=[x_spec(t) for t in range(EMBED_BLOCKS)] + [
            pl.BlockSpec((N_META, D_MODEL), lambda b, n: (0, 0)),
            pl.BlockSpec((1, D_MODEL), lambda b, n: (0, 0)),
            pl.BlockSpec((None, wr, wc), lambda b, n: (0, b, n)),
        ],
        out_specs=[
            pl.BlockSpec((rows, D_MODEL), lambda b, n: (b * steps + n, 0)),
            pl.BlockSpec((rows, D_MODEL), lambda b, n: (b * steps + n, 0)),
            pl.BlockSpec((None, wr, wc), lambda b, n: (n, b, 0)),
        ],
        out_shape=[
            jax.ShapeDtypeStruct((M_ROWS, D_MODEL), F32),
            jax.ShapeDtypeStruct((M_ROWS, D_MODEL), BF16),
            jax.ShapeDtypeStruct((steps, D_MODEL, wc), BF16),
        ],
        compiler_params=pltpu.CompilerParams(
            dimension_semantics=("arbitrary", "arbitrary"),
            vmem_limit_bytes=VMEM_LIMIT),
        name="embed",
    )(*([x] * EMBED_BLOCKS), meta, g, w_in)


def _inproj_kernel(layer_ref, u_ref, w_ref, ca_ref, s1_ref, s2_ref, cr_ref, sr_ref, gain_ref,
                   wg_in, wu_in, wd_in, o_ref, wg_out, wu_out, wd_out):
    del layer_ref
    j = pl.program_id(1)
    tile_row0 = (pl.program_id(0) * TM_IN) % NP

    def tab_rows(r):
        start = tile_row0 + r.start
        start = jnp.where(start >= NP, start - NP, start)
        return pl.ds(pl.multiple_of(start, 8), r.stop - r.start)

    def rope_att(x, r):
        tr = tab_rows(r)
        return (x * ca_ref[tr, :]
                + pltpu.roll(x, ROT_DIM // 2, 1) * s1_ref[tr, :]
                + pltpu.roll(x, ATT_HEAD_DIM - ROT_DIM // 2, 1) * s2_ref[tr, :])

    def slab(acc, c):
        return acc[:, c * SLAB:(c + 1) * SLAB]

    def epi_att(n_heads, scale):
        def epi(acc, r):
            for c in range(TN_IN // SLAB):
                y = slab(acc, c)
                if c < n_heads:
                    y = rope_att(y, r)
                    if scale != 1.0:
                        y = y * scale
                o_ref[c, r, :] = y.astype(BF16)
        return epi

    def epi_ret(scale):
        def epi(acc, r):
            c = cr_ref[tab_rows(r), :]
            s = sr_ref[tab_rows(r), :]
            for hh in range(TN_IN // RET_HEAD_DIM):
                x1 = slab(acc, 2 * hh)
                x2 = slab(acc, 2 * hh + 1)
                y1 = x1 * c - x2 * s
                y2 = x2 * c + x1 * s
                if scale != 1.0:
                    y1 = y1 * scale
                    y2 = y2 * scale
                o_ref[2 * hh, r, :] = y1.astype(BF16)
                o_ref[2 * hh + 1, r, :] = y2.astype(BF16)
        return epi

    def epi_plain(acc, r):
        for c in range(TN_IN // SLAB):
            o_ref[c, r, :] = slab(acc, c).astype(BF16)

    def epi_gate(acc, r):
        y = acc * (1.0 / (1.0 + jnp.exp(-acc))) * gain_ref[...]
        for c in range(TN_IN // SLAB):
            o_ref[c, r, :] = slab(y, c).astype(BF16)

    def run(epi):
        wg_out[...] = wg_in[...].astype(BF16)
        wu_out[...] = wu_in[...].astype(BF16)
        wd_out[...] = wd_in[...].astype(BF16)
        for k in range(TM_IN // IN_CHUNK):
            r = slice(k * IN_CHUNK, (k + 1) * IN_CHUNK)
            epi(jnp.dot(u_ref[r, :], w_ref[...], preferred_element_type=F32), r)

    n_aq = ATT_WIDTH // TN_IN
    j_akv = n_aq
    j_rq = COL_RQ // TN_IN
    j_rk = COL_RK // TN_IN
    j_rv = COL_RV // TN_IN
    j_rg = COL_RG // TN_IN

    @pl.when(j < n_aq)
    def _():
        run(epi_att(TN_IN // ATT_HEAD_DIM, ATT_QSCALE))

    @pl.when(j == j_akv)
    def _():
        run(epi_att(ATT_KV_HEADS, 1.0))

    @pl.when((j >= j_rq) & (j < j_rk))
    def _():
        run(epi_ret(1.0))

    @pl.when((j >= j_rk) & (j < j_rv))
    def _():
        run(epi_ret(RET_HEAD_DIM ** -0.5))

    @pl.when((j >= j_rv) & (j < j_rg))
    def _():
        run(epi_plain)

    @pl.when(j >= j_rg)
    def _():
        run(epi_gate)


def _inproj(u, w, layer, tabs, gain, w_gate, w_up, w_down):
    tab_spec = pl.BlockSpec((NP, 128), lambda i, j, l: (0, 0))
    n_i = M_ROWS // TM_IN
    n_j = IN_COLS // TN_IN
    assert n_i >= CAST_TILES and n_j * TF_FFN == D_FF
    ct = D_MODEL // CAST_TILES
    tile = lambda i: jnp.minimum(i, CAST_TILES - 1)
    return pl.pallas_call(
        _inproj_kernel,
        grid_spec=pltpu.PrefetchScalarGridSpec(
            num_scalar_prefetch=1,
            grid=(n_i, n_j),
            in_specs=[
                pl.BlockSpec((TM_IN, D_MODEL), lambda i, j, l: (i, 0)),
                pl.BlockSpec((None, D_MODEL, TN_IN), lambda i, j, l: (j, 0, 0)),
                tab_spec, tab_spec, tab_spec, tab_spec, tab_spec,
                pl.BlockSpec((1, TN_IN), lambda i, j, l: (0, jnp.maximum(j - COL_RG // TN_IN, 0))),
                pl.BlockSpec((None, ct, TF_FFN), lambda i, j, l: (l[0], tile(i), j)),
                pl.BlockSpec((None, ct, TF_FFN), lambda i, j, l: (l[0], tile(i), j)),
                pl.BlockSpec((None, TF_FFN, ct), lambda i, j, l: (l[0], j, tile(i))),
            ],
            out_specs=[
                pl.BlockSpec((TN_IN // SLAB, TM_IN, SLAB), lambda i, j, l: (j, i, 0)),
                pl.BlockSpec((None, ct, TF_FFN), lambda i, j, l: (j, i, 0)),
                pl.BlockSpec((None, ct, TF_FFN), lambda i, j, l: (j, i, 0)),
                pl.BlockSpec((TF_FFN, ct), lambda i, j, l: (j, i)),
            ],
        ),
        out_shape=[
            jax.ShapeDtypeStruct((IN_COLS // SLAB, M_ROWS, SLAB), BF16),
            jax.ShapeDtypeStruct((n_j, n_i * ct, TF_FFN), BF16),
            jax.ShapeDtypeStruct((n_j, n_i * ct, TF_FFN), BF16),
            jax.ShapeDtypeStruct((D_FF, n_i * ct), BF16),
        ],
        compiler_params=pltpu.CompilerParams(
            dimension_semantics=("arbitrary", "arbitrary"),
            vmem_limit_bytes=VMEM_LIMIT),
        name="inproj",
    )(layer, u, w, *tabs, gain, w_gate, w_up, w_down)


def _att_mask(n):
    row = lax.broadcasted_iota(jnp.int32, (BLOCK, 4 * BLOCK), 0)
    col = lax.broadcasted_iota(jnp.int32, (BLOCK, 4 * BLOCK), 1)
    qi = n * BLOCK + row
    kj = (n - 1) * BLOCK + col
    is_band = col < 3 * BLOCK
    band = is_band & (jnp.abs(qi - kj) <= WINDOW) & (kj >= PAD_FRONT) & (kj < NP)
    mj = col - 3 * BLOCK
    meta = (~is_band) & (mj >= PAD_FRONT) & (jnp.abs(qi - mj) > WINDOW)
    return band | meta


def _attn_kernel(sink_ref, q_ref, k_ref, v_ref, o_ref, bias_ref):
    kv = pl.program_id(1)
    rows = ATT_GROUP * BLOCK

    head = lax.broadcasted_iota(jnp.int32, (rows, 1), 0) // BLOCK
    sink = jnp.zeros((rows, 1), F32)
    for g in range(ATT_GROUP):
        sink = jnp.where(head == g, sink_ref[kv * ATT_GROUP + g] * LOG2E, sink)

    k_meta = k_ref[0:BLOCK, :]
    v_meta = v_ref[0:BLOCK, :]
    ones = jnp.ones((4 * BLOCK, ATT_HEAD_DIM), BF16)

    bias_ref[...] = jnp.where(_att_mask(2), 0.0, NEG).astype(F32)

    def block(n, bias, own_bias):
        r_own = pl.ds(pl.multiple_of(n * BLOCK, BLOCK), BLOCK)
        r_prev = pl.ds(pl.multiple_of(jnp.maximum(n - 1, 0) * BLOCK, BLOCK), BLOCK)
        r_next = pl.ds(pl.multiple_of(jnp.minimum(n + 1, NB - 1) * BLOCK, BLOCK), BLOCK)
        qs = jnp.concatenate([q_ref[g, r_own, :] for g in range(ATT_GROUP)], axis=0)
        ks = jnp.concatenate([k_ref[r_prev, :], k_ref[r_own, :], k_ref[r_next, :], k_meta], axis=0)
        vs = jnp.concatenate([v_ref[r_prev, :], v_ref[r_own, :], v_ref[r_next, :], v_meta], axis=0)
        vs = jnp.concatenate([vs, ones], axis=1)
        s = lax.dot_general(qs, ks, (((1,), (1,)), ((), ())), preferred_element_type=F32)
        bias4 = jnp.concatenate([bias] * ATT_GROUP, axis=0)
        if own_bias:
            s = s + bias4
        else:
            s = jnp.concatenate([s[:, :BLOCK] + bias4[:, :BLOCK], s[:, BLOCK:2 * BLOCK],
                                 s[:, 2 * BLOCK:] + bias4[:, 2 * BLOCK:]], axis=1)
        m = jnp.maximum(jnp.max(s, axis=-1, keepdims=True), sink)
        p = jnp.exp2(s - m).astype(BF16)
        o = jnp.dot(p, vs, preferred_element_type=F32)
        o = o[:, :ATT_HEAD_DIM] / (o[:, ATT_HEAD_DIM:] + jnp.exp2(sink - m))
        for g in range(ATT_GROUP):
            o_ref[g, r_own, :] = o[g * BLOCK:(g + 1) * BLOCK, :].astype(BF16)

    def edge(n):
        block(n, jnp.where(_att_mask(n), 0.0, NEG).astype(F32), True)

    edge(0)
    edge(1)

    def body(n, carry):
        block(n, bias_ref[...], False)
        return carry

    lax.fori_loop(2, NB - 1, body, 0, unroll=10)
    edge(NB - 1)


def _attention(proj, sink):
    assert ATT_HEAD_DIM == SLAB
    return pl.pallas_call(
        _attn_kernel,
        grid=(BATCH, ATT_KV_HEADS),
        in_specs=[
            pl.BlockSpec(memory_space=pltpu.SMEM),
            pl.BlockSpec((ATT_GROUP, NP, SLAB), lambda b, k: (k, b, 0)),
            pl.BlockSpec((None, NP, SLAB), lambda b, k: (COL_AK // SLAB + k, b, 0)),
            pl.BlockSpec((None, NP, SLAB), lambda b, k: (COL_AV // SLAB + k, b, 0)),
        ],
        out_specs=pl.BlockSpec((ATT_GROUP, NP, SLAB), lambda b, k: (k, b, 0)),
        out_shape=jax.ShapeDtypeStruct((ATT_HEADS, M_ROWS, SLAB), BF16),
        scratch_shapes=[pltpu.VMEM((BLOCK, 4 * BLOCK), F32)],
        compiler_params=pltpu.CompilerParams(
            dimension_semantics=("arbitrary", "arbitrary"),
            vmem_limit_bytes=VMEM_LIMIT),
        name="attention",
    )(sink, proj, proj, proj)


def _ret_kernel(dec_ref, q_ref, k_ref, v_ref, g_ref, wo_ref,
                o_ref, wob_ref, st_ref, sf_ref, sb_ref):
    wob_ref[...] = wo_ref[...].astype(BF16)
    hd = pl.program_id(1)
    lgf = -jnp.exp(jnp.full((1, RET_HEAD_DIM), dec_ref[0, hd], F32))
    lgb = -jnp.exp(jnp.full((1, RET_HEAD_DIM), dec_ref[1, hd], F32))

    def weights(L):
        idx = lax.broadcasted_iota(jnp.int32, (L, RET_HEAD_DIM), 0).astype(F32)
        r = lax.broadcasted_iota(jnp.int32, (L, L), 0)
        c = lax.broadcasted_iota(jnp.int32, (L, L), 1)
        diff = (r - c).astype(F32)
        return dict(
            xi_f=jnp.exp(lgf * (idx + 1.0)).astype(BF16),
            zeta_f=jnp.exp(lgf * (L - 1.0 - idx)).astype(BF16),
            xi_b=jnp.exp(lgb * (L - idx)).astype(BF16),
            zeta_b=jnp.exp(lgb * idx).astype(BF16),
            dec_f=jnp.exp(lgf * float(L)),
            dec_b=jnp.exp(lgb * float(L)),
            dmask=jnp.where(diff >= 0.0,
                            jnp.exp(lgf[:, :L] * jnp.maximum(diff, 0.0)),
                            jnp.exp(lgb[:, :L] * jnp.maximum(-diff, 0.0))))

    w_head = weights(BLOCK)
    w_main = weights(RET_CHUNK)

    def main_rows(c):
        return pl.ds(pl.multiple_of(BLOCK + (c - 1) * RET_CHUNK, BLOCK), RET_CHUNK)

    head_rows = slice(0, BLOCK)

    def rd(ref, rows):
        return jnp.concatenate([ref[0, rows, :], ref[1, rows, :]], axis=1)

    def kv_outer(rows, zeta):
        return lax.dot_general(rd(k_ref, rows) * zeta, rd(v_ref, rows), (((0,), (0,)), ((), ())),
                               preferred_element_type=F32)

    zeros = jnp.zeros((RET_HEAD_DIM, RET_HEAD_DIM), F32)
    st_ref[0, 0:RET_HEAD_DIM, :] = zeros.astype(BF16)
    sf_ref[...] = kv_outer(head_rows, w_head["zeta_f"])
    sb_ref[...] = zeros

    def scan(t, carry):
        cf = 1 + t
        cb = RET_CHUNKS - t
        st_ref[cf, 0:RET_HEAD_DIM, :] = sf_ref[...].astype(BF16)
        st_ref[cb, RET_HEAD_DIM:, :] = sb_ref[...].astype(BF16)
        sf_ref[...] = sf_ref[...] * w_main["dec_f"] + kv_outer(main_rows(cf), w_main["zeta_f"])
        sb_ref[...] = sb_ref[...] * w_main["dec_b"] + kv_outer(main_rows(cb), w_main["zeta_b"])
        return carry

    lax.fori_loop(0, RET_CHUNKS, scan, 0, unroll=8)
    st_ref[0, RET_HEAD_DIM:, :] = sb_ref[...].astype(BF16)

    def out(rows, c, w):
        q = rd(q_ref, rows)
        k = rd(k_ref, rows)
        v = rd(v_ref, rows)
        a = lax.dot_general(q, k, (((1,), (1,)), ((), ())), preferred_element_type=F32)
        p = (a * w["dmask"]).astype(BF16)
        qc = jnp.concatenate([q * w["xi_f"], q * w["xi_b"]], axis=1)
        y = (jnp.dot(p, v, preferred_element_type=F32)
             + jnp.dot(qc, st_ref[c], preferred_element_type=F32))
        y = y * lax.rsqrt(jnp.mean(y * y, axis=-1, keepdims=True) + EPS)
        for c in range(RET_HEAD_DIM // SLAB):
            yc = y[:, c * SLAB:(c + 1) * SLAB] * g_ref[c, rows, :].astype(F32)
            o_ref[c, rows, :] = yc.astype(BF16)

    out(head_rows, 0, w_head)

    def out_main(t, carry):
        c = 1 + t
        out(main_rows(c), c, w_main)
        return carry

    lax.fori_loop(0, RET_CHUNKS, out_main, 0, unroll=16)


def _retention(proj, dec, w_out, layer):
    head_slabs = RET_HEAD_DIM // SLAB

    def col_spec(col0):
        return pl.BlockSpec((head_slabs, NP, SLAB), lambda b, h: (col0 // RET_HEAD_DIM + h, b, 0))

    wr = D_MODEL // (BATCH * RET_HEADS)
    return pl.pallas_call(
        _ret_kernel,
        grid=(BATCH, RET_HEADS),
        in_specs=[
            pl.BlockSpec(memory_space=pltpu.SMEM),
            col_spec(COL_RQ), col_spec(COL_RK), col_spec(COL_RV), col_spec(COL_RG),
            pl.BlockSpec((None, wr, D_MODEL), lambda b, h: (layer, b * RET_HEADS + h, 0)),
        ],
        out_specs=[
            pl.BlockSpec((head_slabs, NP, SLAB), lambda b, h: (h, b, 0)),
            pl.BlockSpec((wr, D_MODEL), lambda b, h: (b * RET_HEADS + h, 0)),
        ],
        out_shape=[
            jax.ShapeDtypeStruct((RET_WIDTH // SLAB, M_ROWS, SLAB), BF16),
            jax.ShapeDtypeStruct((D_MODEL, D_MODEL), BF16),
        ],
        scratch_shapes=[
            pltpu.VMEM((RET_CHUNKS + 1, 2 * RET_HEAD_DIM, RET_HEAD_DIM), BF16),
            pltpu.VMEM((RET_HEAD_DIM, RET_HEAD_DIM), F32),
            pltpu.VMEM((RET_HEAD_DIM, RET_HEAD_DIM), F32),
        ],
        compiler_params=pltpu.CompilerParams(
            dimension_semantics=("arbitrary", "arbitrary"),
            vmem_limit_bytes=VMEM_LIMIT),
        name="retention",
    )(dec, proj, proj, proj, proj, w_out)


def _post_norms(y, h_rows, gpost, gnext, valid):
    hn = h_rows + _rms(y, gpost)
    un = _rms(hn, gnext)
    if valid is not None:
        un = jnp.where(valid, un, 0.0)
    return hn, un.astype(BF16)


def _outproj_kernel(att_ref, ret_ref, w_ref, gpost_ref, gffn_ref, h_ref, ho_ref, u_ref):
    n_chunks = TM_OUT // OUT_CHUNK
    rows = lambda k: slice(k * OUT_CHUNK, (k + 1) * OUT_CHUNK)

    def project(k):
        r = rows(k)
        mixed = jnp.concatenate(
            [att_ref[c, r, :] for c in range(ATT_WIDTH // SLAB)]
            + [ret_ref[c, r, :] for c in range(RET_WIDTH // SLAB)], axis=1)
        return jnp.dot(mixed, w_ref[...], preferred_element_type=F32)

    y = project(0)
    for k in range(n_chunks):
        y_next = project(k + 1) if k + 1 < n_chunks else None
        hn, un = _post_norms(y, h_ref[rows(k), :], gpost_ref[...], gffn_ref[...], None)
        ho_ref[rows(k), :] = hn
        u_ref[rows(k), :] = un
        y = y_next


def _outproj(att, ret, w, gpost, gffn, h):
    row = lambda i: (i, 0)
    const = lambda i: (0, 0)
    return pl.pallas_call(
        _outproj_kernel,
        grid=(M_ROWS // TM_OUT,),
        in_specs=[
            pl.BlockSpec((ATT_WIDTH // SLAB, TM_OUT, SLAB), lambda i: (0, i, 0)),
            pl.BlockSpec((RET_WIDTH // SLAB, TM_OUT, SLAB), lambda i: (0, i, 0)),
            pl.BlockSpec((D_MODEL, D_MODEL), const),
            pl.BlockSpec((1, D_MODEL), const),
            pl.BlockSpec((1, D_MODEL), const),
            pl.BlockSpec((TM_OUT, D_MODEL), row),
        ],
        out_specs=[
            pl.BlockSpec((TM_OUT, D_MODEL), row),
            pl.BlockSpec((TM_OUT, D_MODEL), row),
        ],
        out_shape=[
            jax.ShapeDtypeStruct((M_ROWS, D_MODEL), F32),
            jax.ShapeDtypeStruct((M_ROWS, D_MODEL), BF16),
        ],
        input_output_aliases={5: 0},
        compiler_params=pltpu.CompilerParams(
            dimension_semantics=("arbitrary",),
            vmem_limit_bytes=VMEM_LIMIT),
        name="outproj",
    )(att, ret, w, gpost, gffn, h)


def _ffn_steps(u_ref, wg_ref, wu_ref, wd_ref, acc_ref, finish, side_job=lambda: None):
    j = pl.program_id(1)
    last = pl.num_programs(1) - 1

    def swiglu(rows):
        u = u_ref[rows, :]
        g = jnp.dot(u, wg_ref[...], preferred_element_type=F32)
        up = jnp.dot(u, wu_ref[...], preferred_element_type=F32)
        f = (g * (1.0 / (1.0 + jnp.exp(-g))) * up).astype(BF16)
        return jnp.dot(f, wd_ref[...], preferred_element_type=F32)

    @pl.when(j == 0)
    def _():
        acc_ref[...] = jnp.zeros((TM_FFN, D_MODEL), F32)

    @pl.when(j < last)
    def _():
        side_job()
        acc_ref[...] += swiglu(slice(None))

    @pl.when(j == last)
    def _():
        side_job()
        n_chunks = TM_FFN // FFN_CHUNK
        rows = lambda k: slice(k * FFN_CHUNK, (k + 1) * FFN_CHUNK)
        y_next = acc_ref[rows(0), :] + swiglu(rows(0))
        for k in range(n_chunks):
            y = y_next
            if k + 1 < n_chunks:
                y_next = acc_ref[rows(k + 1), :] + swiglu(rows(k + 1))
            finish(k, rows(k), y)


def _ffn_mid_kernel(layer_ref, u_ref, wg_ref, wu_ref, wd_ref, gpost_ref, gnext_ref, h_ref, win_ref,
                    ho_ref, un_ref, winb_ref):
    del layer_ref
    i = pl.program_id(0)

    def cast_next_w_in_tile():
        winb_ref[...] = win_ref[...].astype(BF16)

    def finish(k, r, y):
        valid = _valid_rows(i * TM_FFN + k * FFN_CHUNK, FFN_CHUNK)
        hn, un = _post_norms(y, h_ref[r, :], gpost_ref[...], gnext_ref[...], valid)
        ho_ref[r, :] = hn
        un_ref[r, :] = un

    _ffn_steps(u_ref, wg_ref, wu_ref, wd_ref, ho_ref, finish, cast_next_w_in_tile)


def _ffn_mid(u, wg, wu, wd, gpost, gnext, h, w_in, next_layer):
    row = lambda i, j, l: (i, 0)
    const = lambda i, j, l: (0, 0)
    n_i = M_ROWS // TM_FFN
    n_j = D_FF // TF_FFN
    assert n_i >= IN_CAST_TILES and n_j * TN_IN == IN_COLS
    ct = D_MODEL // IN_CAST_TILES
    tile = lambda i: jnp.minimum(i, IN_CAST_TILES - 1)
    return pl.pallas_call(
        _ffn_mid_kernel,
        grid_spec=pltpu.PrefetchScalarGridSpec(
            num_scalar_prefetch=1,
            grid=(n_i, n_j),
            in_specs=[
                pl.BlockSpec((TM_FFN, D_MODEL), row),
                pl.BlockSpec((None, D_MODEL, TF_FFN), lambda i, j, l: (j, 0, 0)),
                pl.BlockSpec((None, D_MODEL, TF_FFN), lambda i, j, l: (j, 0, 0)),
                pl.BlockSpec((TF_FFN, D_MODEL), lambda i, j, l: (j, 0)),
                pl.BlockSpec((1, D_MODEL), const),
                pl.BlockSpec((1, D_MODEL), const),
                pl.BlockSpec((TM_FFN, D_MODEL), row),
                pl.BlockSpec((None, ct, TN_IN), lambda i, j, l: (l[0], tile(i), j)),
            ],
            out_specs=[
                pl.BlockSpec((TM_FFN, D_MODEL), row),
                pl.BlockSpec((TM_FFN, D_MODEL), row),
                pl.BlockSpec((None, ct, TN_IN), lambda i, j, l: (j, i, 0)),
            ],
        ),
        out_shape=[
            jax.ShapeDtypeStruct((M_ROWS, D_MODEL), F32),
            jax.ShapeDtypeStruct((M_ROWS, D_MODEL), BF16),
            jax.ShapeDtypeStruct((n_j, n_i * ct, TN_IN), BF16),
        ],
        input_output_aliases={7: 0},
        compiler_params=pltpu.CompilerParams(
            dimension_semantics=("arbitrary", "arbitrary"),
            vmem_limit_bytes=VMEM_LIMIT),
        name="ffn",
    )(next_layer, u, wg, wu, wd, gpost, gnext, h, w_in)


def _token_block_copies(tile, acc_ref, slot, out_ref, sem):
    copies = []
    for q in range(TM_FFN // BLOCK):
        gb = tile * (TM_FFN // BLOCK) + q
        b = sum((gb >= k * NB).astype(jnp.int32) for k in range(1, BATCH))
        n = gb - b * NB
        dst = pl.multiple_of((b * (SEQ // BLOCK) + jnp.maximum(n - 1, 0)) * BLOCK, BLOCK)
        copy = pltpu.make_async_copy(
            acc_ref.at[slot, pl.ds(q * BLOCK, BLOCK), :], out_ref.at[pl.ds(dst, BLOCK), :],
            sem.at[slot])
        copies.append((n >= 1, copy))
    return copies


def _ffn_last_kernel(u_ref, wg_ref, wu_ref, wd_ref, gpost_ref, h_ref, out_ref, acc_ref, sem):
    i = pl.program_id(0)
    j = pl.program_id(1)
    slot = i % 2
    acc = acc_ref.at[slot]

    def finish(k, r, y):
        acc[r, :] = h_ref[r, :] + _rms(y, gpost_ref[...])

    _ffn_steps(u_ref, wg_ref, wu_ref, wd_ref, acc, finish)

    @pl.when(j == pl.num_programs(1) - 1)
    def _():
        for is_token, copy in _token_block_copies(i, acc_ref, slot, out_ref, sem):
            @pl.when(is_token)
            def _():
                copy.start()

        @pl.when(i > 0)
        def _():
            for is_token, copy in _token_block_copies(i - 1, acc_ref, 1 - slot, out_ref, sem):
                @pl.when(is_token)
                def _():
                    copy.wait()

        @pl.when(i == pl.num_programs(0) - 1)
        def _():
            for is_token, copy in _token_block_copies(i, acc_ref, slot, out_ref, sem):
                @pl.when(is_token)
                def _():
                    copy.wait()


def _ffn_last(u, wg, wu, wd, gpost, h):
    row = lambda i, j: (i, 0)
    const = lambda i, j: (0, 0)
    return pl.pallas_call(
        _ffn_last_kernel,
        grid=(M_ROWS // TM_FFN, D_FF // TF_FFN),
        in_specs=[
            pl.BlockSpec((TM_FFN, D_MODEL), row),
            pl.BlockSpec((None, D_MODEL, TF_FFN), lambda i, j: (j, 0, 0)),
            pl.BlockSpec((None, D_MODEL, TF_FFN), lambda i, j: (j, 0, 0)),
            pl.BlockSpec((TF_FFN, D_MODEL), lambda i, j: (j, 0)),
            pl.BlockSpec((1, D_MODEL), const),
            pl.BlockSpec((TM_FFN, D_MODEL), row),
        ],
        out_specs=pl.BlockSpec(memory_space=pl.ANY),
        out_shape=jax.ShapeDtypeStruct((BATCH * SEQ, D_MODEL), F32),
        scratch_shapes=[
            pltpu.VMEM((2, TM_FFN, D_MODEL), F32),
            pltpu.SemaphoreType.DMA((2,)),
        ],
        compiler_params=pltpu.CompilerParams(
            dimension_semantics=("arbitrary", "arbitrary"),
            vmem_limit_bytes=VMEM_LIMIT),
        name="ffn_last",
    )(u, wg, wu, wd, gpost, h)


def _rope_tables():
    pos = (jnp.arange(NP) - PAD_FRONT).astype(F32)
    half_a = ROT_DIM // 2
    inv_a = ROPE_THETA ** (-jnp.arange(half_a, dtype=F32) / half_a)
    ang_a = pos[:, None] * inv_a[None, :]
    cos_a, sin_a = jnp.cos(ang_a), jnp.sin(ang_a)
    rest = ATT_HEAD_DIM - ROT_DIM
    ca = jnp.concatenate([cos_a, cos_a, jnp.ones((NP, rest), F32)], axis=1)
    s1 = jnp.concatenate([jnp.zeros((NP, half_a), F32), sin_a, jnp.zeros((NP, rest), F32)], axis=1)
    s2 = jnp.concatenate([-sin_a, jnp.zeros((NP, ATT_HEAD_DIM - half_a), F32)], axis=1)
    half_r = RET_HEAD_DIM // 2
    inv_r = RET_THETA ** (-jnp.arange(half_r, dtype=F32) / half_r)
    ang_r = pos[:, None] * inv_r[None, :]
    return ca, s1, s2, jnp.cos(ang_r), jnp.sin(ang_r)


def kernel(x, meta_tokens, w_in, w_out, attn_sink, ret_decay_fwd, ret_decay_bwd, ret_norm,
           norm_mix_pre, norm_mix_post, w_gate, w_up, w_down, norm_ffn_pre, norm_ffn_post):
    tabs = _rope_tables()
    row = lambda a, l: a[l].reshape(1, -1).astype(F32)
    w_in, w_out, w_gate, w_up, w_down = (
        w.astype(F32) for w in (w_in, w_out, w_gate, w_up, w_down))
    h, u, w_in_l = _embed(x, meta_tokens, row(norm_mix_pre, 0), w_in)
    for l in range(DEPTH):
        layer = jnp.full((1,), l, jnp.int32)
        proj, wg, wu, wd = _inproj(u, w_in_l, layer, tabs, row(ret_norm, l), w_gate, w_up, w_down)
        att = _attention(proj, attn_sink[l].astype(F32))
        dec = jnp.stack([ret_decay_fwd[l], ret_decay_bwd[l]]).astype(F32)
        ret, w_out_l = _retention(proj, dec, w_out, l)
        h, u = _outproj(att, ret, w_out_l, row(norm_mix_post, l), row(norm_ffn_pre, l), h)
        if l + 1 < DEPTH:
            h, u, w_in_l = _ffn_mid(u, wg, wu, wd, row(norm_ffn_post, l), row(norm_mix_pre, l + 1),
                                    h, w_in, jnp.full((1,), l + 1, jnp.int32))
        else:
            out = _ffn_last(u, wg, wu, wd, row(norm_ffn_post, l), h)
    return out.reshape(BATCH, SEQ, D_MODEL)
```

```python
import functools

import jax
import jax.numpy as jnp
from jax import lax
from jax.experimental import pallas as pl
from jax.experimental.pallas import tpu as pltpu

F32 = jnp.float32
BF16 = jnp.bfloat16

D_MODEL = 2048
BATCH = 4
SEQ = 4096
DEPTH = 4
N_META = 16
BLOCK = 128
SLAB = 128
WINDOW = 128
PAD_FRONT = BLOCK - N_META
NP = PAD_FRONT + N_META + SEQ
NB = NP // BLOCK
M_ROWS = BATCH * NP
ATT_HEAD_DIM = 128
ATT_WIDTH = 1024
ATT_HEADS = 8
ATT_KV_HEADS = 2
ATT_GROUP = ATT_HEADS // ATT_KV_HEADS
KV_WIDTH = ATT_KV_HEADS * ATT_HEAD_DIM
ROT_DIM = 32
ROPE_THETA = 500000.0
RET_WIDTH = 1024
RET_HEAD_DIM = 256
RET_HEADS = 4
RET_THETA = 10000.0
D_FF = 5632
IN_COLS = ATT_WIDTH + 2 * KV_WIDTH + 4 * RET_WIDTH
EPS = 1e-6
NEG = -1e30
LOG2E = 1.4426950408889634
ATT_QSCALE = ATT_HEAD_DIM ** -0.5 * LOG2E

COL_AQ = 0
COL_AK = ATT_WIDTH
COL_AV = COL_AK + KV_WIDTH
COL_RQ = COL_AV + KV_WIDTH
COL_RK = COL_RQ + RET_WIDTH
COL_RV = COL_RK + RET_WIDTH
COL_RG = COL_RV + RET_WIDTH

VMEM_LIMIT = 56 * 1024 * 1024

TM_IN = 1408
TN_IN = 512
TM_OUT = 768
TM_FFN = 768
TF_FFN = 512
IN_CAST_TILES = 16
CAST_TILES = 8
EMBED_BLOCKS = 3
RET_CHUNK = 256
RET_CHUNKS = (NP - BLOCK) // RET_CHUNK
IN_CHUNKS = (176,) * 8
OUT_CHUNKS = (192, 192, 192, 192)
FFN_CHUNKS = (256, 256, 256)
assert sum(IN_CHUNKS) == TM_IN and sum(OUT_CHUNKS) == TM_OUT and sum(FFN_CHUNKS) == TM_FFN


def _chunk_slices(sizes):
    bounds = [sum(sizes[:k]) for k in range(len(sizes) + 1)]
    return [slice(a, b) for a, b in zip(bounds[:-1], bounds[1:])]


def _rms(x, g):
    return x * lax.rsqrt(jnp.mean(x * x, axis=-1, keepdims=True) + EPS) * g


def _valid_rows(row0, rows):
    r = row0 + lax.broadcasted_iota(jnp.int32, (rows, 1), 0)
    ok = r >= 0
    for b in range(BATCH):
        ok = ok & ~((r >= b * NP) & (r < b * NP + PAD_FRONT))
    return ok


def _embed_kernel(*refs):
    x_refs = refs[:EMBED_BLOCKS]
    meta_ref, g_ref, w_ref, h_ref, u_ref, wb_ref = refs[EMBED_BLOCKS:]
    n = pl.program_id(1)
    wb_ref[...] = w_ref[...].astype(BF16)

    def copy(x_ref, rows):
        xv = x_ref[0]
        h_ref[rows, :] = xv
        u_ref[rows, :] = _rms(xv, g_ref[...]).astype(BF16)

    @pl.when(n == 0)
    def _():
        h_ref[0:PAD_FRONT, :] = jnp.zeros((PAD_FRONT, D_MODEL), F32)
        u_ref[0:PAD_FRONT, :] = jnp.zeros((PAD_FRONT, D_MODEL), BF16)
        m = meta_ref[...]
        h_ref[PAD_FRONT:BLOCK, :] = m
        u_ref[PAD_FRONT:BLOCK, :] = _rms(m, g_ref[...]).astype(BF16)

    @pl.when(n > 0)
    def _():
        copy(x_refs[0], slice(0, BLOCK))

    for t in range(1, EMBED_BLOCKS):
        copy(x_refs[t], slice(t * BLOCK, (t + 1) * BLOCK))


def _embed(x, meta, g, w_in):
    rows = EMBED_BLOCKS * BLOCK
    steps = NB // EMBED_BLOCKS
    wr, wc = D_MODEL // BATCH, IN_COLS // steps
    assert wr * BATCH == D_MODEL and wc * steps == IN_COLS and wc == TN_IN

    def x_spec(t):
        return pl.BlockSpec(
            (1, BLOCK, D_MODEL), lambda b, n: (b, jnp.maximum(n * EMBED_BLOCKS + t - 1, 0), 0))

    return pl.pallas_call(
        _embed_kernel,
        grid=(BATCH, steps),
        in_specs=[x_spec(t) for t in range(EMBED_BLOCKS)] + [
            pl.BlockSpec((N_META, D_MODEL), lambda b, n: (0, 0)),
            pl.BlockSpec((1, D_MODEL), lambda b, n: (0, 0)),
            pl.BlockSpec((None, wr, wc), lambda b, n: (0, b, n)),
        ],
        out_specs=[
            pl.BlockSpec((rows, D_MODEL), lambda b, n: (b * steps + n, 0)),
            pl.BlockSpec((rows, D_MODEL), lambda b, n: (b * steps + n, 0)),
            pl.BlockSpec((None, wr, wc), lambda b, n: (n, b, 0)),
        ],
        out_shape=[
            jax.ShapeDtypeStruct((M_ROWS, D_MODEL), F32),
            jax.ShapeDtypeStruct((M_ROWS, D_MODEL), BF16),
            jax.ShapeDtypeStruct((steps, D_MODEL, wc), BF16),
        ],
        compiler_params=pltpu.CompilerParams(
            dimension_semantics=("arbitrary", "arbitrary"),
            vmem_limit_bytes=VMEM_LIMIT),
        name="embed",
    )(*([x] * EMBED_BLOCKS), meta, g, w_in)


def _inproj_kernel(layer_ref, u_ref, w_ref, ca_ref, s1_ref, s2_ref, cr_ref, sr_ref, gain_ref,
                   wg_in, wu_in, wd_in, o_ref, wg_out, wu_out, wd_out):
    del layer_ref
    j = pl.program_id(1)
    tile_row0 = (pl.program_id(0) * TM_IN) % NP

    def tab_rows(r):
        start = tile_row0 + r.start
        start = jnp.where(start >= NP, start - NP, start)
        return pl.ds(pl.multiple_of(start, 8), r.stop - r.start)

    def rope_att(x, r):
        tr = tab_rows(r)
        return (x * ca_ref[tr, :]
                + pltpu.roll(x, ROT_DIM // 2, 1) * s1_ref[tr, :]
                + pltpu.roll(x, ATT_HEAD_DIM - ROT_DIM // 2, 1) * s2_ref[tr, :])

    def slab(acc, c):
        return acc[:, c * SLAB:(c + 1) * SLAB]

    def epi_att(n_heads, scale):
        def epi(acc, r):
            for c in range(TN_IN // SLAB):
                y = slab(acc, c)
                if c < n_heads:
                    y = rope_att(y, r)
                    if scale != 1.0:
                        y = y * scale
                o_ref[c, r, :] = y.astype(BF16)
        return epi

    def epi_ret(scale):
        def epi(acc, r):
            c = cr_ref[tab_rows(r), :]
            s = sr_ref[tab_rows(r), :]
            for hh in range(TN_IN // RET_HEAD_DIM):
                x1 = slab(acc, 2 * hh)
                x2 = slab(acc, 2 * hh + 1)
                y1 = x1 * c - x2 * s
                y2 = x2 * c + x1 * s
                if scale != 1.0:
                    y1 = y1 * scale
                    y2 = y2 * scale
                o_ref[2 * hh, r, :] = y1.astype(BF16)
                o_ref[2 * hh + 1, r, :] = y2.astype(BF16)
        return epi

    def epi_plain(acc, r):
        for c in range(TN_IN // SLAB):
            o_ref[c, r, :] = slab(acc, c).astype(BF16)

    def epi_gate(acc, r):
        y = acc * (1.0 / (1.0 + jnp.exp(-acc))) * gain_ref[...]
        for c in range(TN_IN // SLAB):
            o_ref[c, r, :] = slab(y, c).astype(BF16)

    def run(epi):
        wg_out[...] = wg_in[...].astype(BF16)
        wu_out[...] = wu_in[...].astype(BF16)
        wd_out[...] = wd_in[...].astype(BF16)
        for r in _chunk_slices(IN_CHUNKS):
            epi(jnp.dot(u_ref[r, :], w_ref[...], preferred_element_type=F32), r)

    n_aq = ATT_WIDTH // TN_IN
    j_akv = n_aq
    j_rq = COL_RQ // TN_IN
    j_rk = COL_RK // TN_IN
    j_rv = COL_RV // TN_IN
    j_rg = COL_RG // TN_IN

    @pl.when(j < n_aq)
    def _():
        run(epi_att(TN_IN // ATT_HEAD_DIM, ATT_QSCALE))

    @pl.when(j == j_akv)
    def _():
        run(epi_att(ATT_KV_HEADS, 1.0))

    @pl.when((j >= j_rq) & (j < j_rk))
    def _():
        run(epi_ret(1.0))

    @pl.when((j >= j_rk) & (j < j_rv))
    def _():
        run(epi_ret(RET_HEAD_DIM ** -0.5))

    @pl.when((j >= j_rv) & (j < j_rg))
    def _():
        run(epi_plain)

    @pl.when(j >= j_rg)
    def _():
        run(epi_gate)


def _inproj(u, w, layer, tabs, gain, w_gate, w_up, w_down):
    tab_spec = pl.BlockSpec((NP, 128), lambda i, j, l: (0, 0))
    n_i = M_ROWS // TM_IN
    n_j = IN_COLS // TN_IN
    assert n_i >= CAST_TILES and n_j * TF_FFN == D_FF
    ct = D_MODEL // CAST_TILES
    tile = lambda i: jnp.minimum(i, CAST_TILES - 1)
    return pl.pallas_call(
        _inproj_kernel,
        grid_spec=pltpu.PrefetchScalarGridSpec(
            num_scalar_prefetch=1,
            grid=(n_i, n_j),
            in_specs=[
                pl.BlockSpec((TM_IN, D_MODEL), lambda i, j, l: (i, 0)),
                pl.BlockSpec((None, D_MODEL, TN_IN), lambda i, j, l: (j, 0, 0)),
                tab_spec, tab_spec, tab_spec, tab_spec, tab_spec,
                pl.BlockSpec((1, TN_IN), lambda i, j, l: (0, jnp.maximum(j - COL_RG // TN_IN, 0))),
                pl.BlockSpec((None, ct, TF_FFN), lambda i, j, l: (l[0], tile(i), j)),
                pl.BlockSpec((None, ct, TF_FFN), lambda i, j, l: (l[0], tile(i), j)),
                pl.BlockSpec((None, TF_FFN, ct), lambda i, j, l: (l[0], j, tile(i))),
            ],
            out_specs=[
                pl.BlockSpec((TN_IN // SLAB, TM_IN, SLAB), lambda i, j, l: (j, i, 0)),
                pl.BlockSpec((None, ct, TF_FFN), lambda i, j, l: (j, i, 0)),
                pl.BlockSpec((None, ct, TF_FFN), lambda i, j, l: (j, i, 0)),
                pl.BlockSpec((TF_FFN, ct), lambda i, j, l: (j, i)),
            ],
        ),
        out_shape=[
            jax.ShapeDtypeStruct((IN_COLS // SLAB, M_ROWS, SLAB), BF16),
            jax.ShapeDtypeStruct((n_j, n_i * ct, TF_FFN), BF16),
            jax.ShapeDtypeStruct((n_j, n_i * ct, TF_FFN), BF16),
            jax.ShapeDtypeStruct((D_FF, n_i * ct), BF16),
        ],
        compiler_params=pltpu.CompilerParams(
            dimension_semantics=("arbitrary", "arbitrary"),
            vmem_limit_bytes=VMEM_LIMIT),
        name="inproj",
    )(layer, u, w, *tabs, gain, w_gate, w_up, w_down)


def _att_mask(n):
    row = lax.broadcasted_iota(jnp.int32, (BLOCK, 4 * BLOCK), 0)
    col = lax.broadcasted_iota(jnp.int32, (BLOCK, 4 * BLOCK), 1)
    qi = n * BLOCK + row
    kj = (n - 1) * BLOCK + col
    is_band = col < 3 * BLOCK
    band = is_band & (jnp.abs(qi - kj) <= WINDOW) & (kj >= PAD_FRONT) & (kj < NP)
    mj = col - 3 * BLOCK
    meta = (~is_band) & (mj >= PAD_FRONT) & (jnp.abs(qi - mj) > WINDOW)
    return band | meta


def _attn_kernel(sink_ref, q_ref, k_ref, v_ref, o_ref, bias_ref):
    kv = pl.program_id(1)
    rows = ATT_GROUP * BLOCK

    head = lax.broadcasted_iota(jnp.int32, (rows, 1), 0) // BLOCK
    sink = jnp.zeros((rows, 1), F32)
    for g in range(ATT_GROUP):
        sink = jnp.where(head == g, sink_ref[kv * ATT_GROUP + g] * LOG2E, sink)

    k_meta = k_ref[0:BLOCK, :]
    v_meta = v_ref[0:BLOCK, :]
    ones = jnp.ones((4 * BLOCK, ATT_HEAD_DIM), BF16)

    bias_ref[...] = jnp.where(_att_mask(2), 0.0, NEG).astype(F32)

    def block(n, bias, own_bias):
        r_own = pl.ds(pl.multiple_of(n * BLOCK, BLOCK), BLOCK)
        r_prev = pl.ds(pl.multiple_of(jnp.maximum(n - 1, 0) * BLOCK, BLOCK), BLOCK)
        r_next = pl.ds(pl.multiple_of(jnp.minimum(n + 1, NB - 1) * BLOCK, BLOCK), BLOCK)
        qs = jnp.concatenate([q_ref[g, r_own, :] for g in range(ATT_GROUP)], axis=0)
        ks = jnp.concatenate([k_ref[r_prev, :], k_ref[r_own, :], k_ref[r_next, :], k_meta], axis=0)
        vs = jnp.concatenate([v_ref[r_prev, :], v_ref[r_own, :], v_ref[r_next, :], v_meta], axis=0)
        vs = jnp.concatenate([vs, ones], axis=1)
        s = lax.dot_general(qs, ks, (((1,), (1,)), ((), ())), preferred_element_type=F32)
        bias4 = jnp.concatenate([bias] * ATT_GROUP, axis=0)
        if own_bias:
            s = s + bias4
        else:
            s = jnp.concatenate([s[:, :BLOCK] + bias4[:, :BLOCK], s[:, BLOCK:2 * BLOCK],
                                 s[:, 2 * BLOCK:] + bias4[:, 2 * BLOCK:]], axis=1)
        m = jnp.maximum(jnp.max(s, axis=-1, keepdims=True), sink)
        p = jnp.exp2(s - m).astype(BF16)
        o = jnp.dot(p, vs, preferred_element_type=F32)
        o = o[:, :ATT_HEAD_DIM] / (o[:, ATT_HEAD_DIM:] + jnp.exp2(sink - m))
        for g in range(ATT_GROUP):
            o_ref[g, r_own, :] = o[g * BLOCK:(g + 1) * BLOCK, :].astype(BF16)

    def edge(n):
        block(n, jnp.where(_att_mask(n), 0.0, NEG).astype(F32), True)

    edge(0)
    edge(1)

    def body(n, carry):
        block(n, bias_ref[...], False)
        return carry

    lax.fori_loop(2, NB - 1, body, 0, unroll=10)
    edge(NB - 1)


def _attention(proj, sink):
    assert ATT_HEAD_DIM == SLAB
    return pl.pallas_call(
        _attn_kernel,
        grid=(BATCH, ATT_KV_HEADS),
        in_specs=[
            pl.BlockSpec(memory_space=pltpu.SMEM),
            pl.BlockSpec((ATT_GROUP, NP, SLAB), lambda b, k: (k, b, 0)),
            pl.BlockSpec((None, NP, SLAB), lambda b, k: (COL_AK // SLAB + k, b, 0)),
            pl.BlockSpec((None, NP, SLAB), lambda b, k: (COL_AV // SLAB + k, b, 0)),
        ],
        out_specs=pl.BlockSpec((ATT_GROUP, NP, SLAB), lambda b, k: (k, b, 0)),
        out_shape=jax.ShapeDtypeStruct((ATT_HEADS, M_ROWS, SLAB), BF16),
        scratch_shapes=[pltpu.VMEM((BLOCK, 4 * BLOCK), F32)],
        compiler_params=pltpu.CompilerParams(
            dimension_semantics=("arbitrary", "arbitrary"),
            vmem_limit_bytes=VMEM_LIMIT),
        name="attention",
    )(sink, proj, proj, proj)


def _ret_kernel(dec_ref, q_ref, k_ref, v_ref, g_ref, wo_ref,
                o_ref, wob_ref, st_ref, sf_ref, sb_ref):
    wob_ref[...] = wo_ref[...].astype(BF16)
    hd = pl.program_id(1)
    lgf = -jnp.exp(jnp.full((1, RET_HEAD_DIM), dec_ref[0, hd], F32))
    lgb = -jnp.exp(jnp.full((1, RET_HEAD_DIM), dec_ref[1, hd], F32))

    def weights(L):
        idx = lax.broadcasted_iota(jnp.int32, (L, RET_HEAD_DIM), 0).astype(F32)
        r = lax.broadcasted_iota(jnp.int32, (L, L), 0)
        c = lax.broadcasted_iota(jnp.int32, (L, L), 1)
        diff = (r - c).astype(F32)
        return dict(
            xi_f=jnp.exp(lgf * (idx + 1.0)).astype(BF16),
            zeta_f=jnp.exp(lgf * (L - 1.0 - idx)).astype(BF16),
            xi_b=jnp.exp(lgb * (L - idx)).astype(BF16),
            zeta_b=jnp.exp(lgb * idx).astype(BF16),
            dec_f=jnp.exp(lgf * float(L)),
            dec_b=jnp.exp(lgb * float(L)),
            dmask=jnp.where(diff >= 0.0,
                            jnp.exp(lgf[:, :L] * jnp.maximum(diff, 0.0)),
                            jnp.exp(lgb[:, :L] * jnp.maximum(-diff, 0.0))))

    w_head = weights(BLOCK)
    w_main = weights(RET_CHUNK)

    def main_rows(c):
        return pl.ds(pl.multiple_of(BLOCK + (c - 1) * RET_CHUNK, BLOCK), RET_CHUNK)

    head_rows = slice(0, BLOCK)

    def rd(ref, rows):
        return jnp.concatenate([ref[0, rows, :], ref[1, rows, :]], axis=1)

    def kv_outer(rows, zeta):
        return lax.dot_general(rd(k_ref, rows) * zeta, rd(v_ref, rows), (((0,), (0,)), ((), ())),
                               preferred_element_type=F32)

    zeros = jnp.zeros((RET_HEAD_DIM, RET_HEAD_DIM), F32)
    st_ref[0, 0:RET_HEAD_DIM, :] = zeros.astype(BF16)
    sf_ref[...] = kv_outer(head_rows, w_head["zeta_f"])
    sb_ref[...] = zeros

    def scan(t, carry):
        cf = 1 + t
        cb = RET_CHUNKS - t
        st_ref[cf, 0:RET_HEAD_DIM, :] = sf_ref[...].astype(BF16)
        st_ref[cb, RET_HEAD_DIM:, :] = sb_ref[...].astype(BF16)
        sf_ref[...] = sf_ref[...] * w_main["dec_f"] + kv_outer(main_rows(cf), w_main["zeta_f"])
        sb_ref[...] = sb_ref[...] * w_main["dec_b"] + kv_outer(main_rows(cb), w_main["zeta_b"])
        return carry

    lax.fori_loop(0, RET_CHUNKS, scan, 0, unroll=8)
    st_ref[0, RET_HEAD_DIM:, :] = sb_ref[...].astype(BF16)

    def out(rows, c, w):
        q = rd(q_ref, rows)
        k = rd(k_ref, rows)
        v = rd(v_ref, rows)
        a = lax.dot_general(q, k, (((1,), (1,)), ((), ())), preferred_element_type=F32)
        p = (a * w["dmask"]).astype(BF16)
        qc = jnp.concatenate([q * w["xi_f"], q * w["xi_b"]], axis=1)
        y = (jnp.dot(p, v, preferred_element_type=F32)
             + jnp.dot(qc, st_ref[c], preferred_element_type=F32))
        y = y * lax.rsqrt(jnp.mean(y * y, axis=-1, keepdims=True) + EPS)
        for c in range(RET_HEAD_DIM // SLAB):
            yc = y[:, c * SLAB:(c + 1) * SLAB] * g_ref[c, rows, :].astype(F32)
            o_ref[c, rows, :] = yc.astype(BF16)

    out(head_rows, 0, w_head)

    def out_main(t, carry):
        c = 1 + t
        out(main_rows(c), c, w_main)
        return carry

    lax.fori_loop(0, RET_CHUNKS, out_main, 0, unroll=16)


def _retention(proj, dec, w_out, layer):
    head_slabs = RET_HEAD_DIM // SLAB

    def col_spec(col0):
        return pl.BlockSpec((head_slabs, NP, SLAB), lambda b, h: (col0 // RET_HEAD_DIM + h, b, 0))

    wr = D_MODEL // (BATCH * RET_HEADS)
    return pl.pallas_call(
        _ret_kernel,
        grid=(BATCH, RET_HEADS),
        in_specs=[
            pl.BlockSpec(memory_space=pltpu.SMEM),
            col_spec(COL_RQ), col_spec(COL_RK), col_spec(COL_RV), col_spec(COL_RG),
            pl.BlockSpec((None, wr, D_MODEL), lambda b, h: (layer, b * RET_HEADS + h, 0)),
        ],
        out_specs=[
            pl.BlockSpec((head_slabs, NP, SLAB), lambda b, h: (h, b, 0)),
            pl.BlockSpec((wr, D_MODEL), lambda b, h: (b * RET_HEADS + h, 0)),
        ],
        out_shape=[
            jax.ShapeDtypeStruct((RET_WIDTH // SLAB, M_ROWS, SLAB), BF16),
            jax.ShapeDtypeStruct((D_MODEL, D_MODEL), BF16),
        ],
        scratch_shapes=[
            pltpu.VMEM((RET_CHUNKS + 1, 2 * RET_HEAD_DIM, RET_HEAD_DIM), BF16),
            pltpu.VMEM((RET_HEAD_DIM, RET_HEAD_DIM), F32),
            pltpu.VMEM((RET_HEAD_DIM, RET_HEAD_DIM), F32),
        ],
        compiler_params=pltpu.CompilerParams(
            dimension_semantics=("arbitrary", "arbitrary"),
            vmem_limit_bytes=VMEM_LIMIT),
        name="retention",
    )(dec, proj, proj, proj, proj, w_out)


def _post_norms(y, h_rows, gpost, gnext, valid):
    hn = h_rows + _rms(y, gpost)
    scale = lax.rsqrt(jnp.mean(hn * hn, axis=-1, keepdims=True) + EPS)
    if valid is not None:
        scale = jnp.where(valid, scale, 0.0)
    return hn, (hn * scale * gnext).astype(BF16)


def _outproj_kernel(att_ref, ret_ref, w_ref, gpost_ref, gffn_ref, h_ref, ho_ref, u_ref):
    chunks = _chunk_slices(OUT_CHUNKS)
    n_chunks = len(chunks)
    rows = lambda k: chunks[k]

    def project(k):
        r = rows(k)
        mixed = jnp.concatenate(
            [att_ref[c, r, :] for c in range(ATT_WIDTH // SLAB)]
            + [ret_ref[c, r, :] for c in range(RET_WIDTH // SLAB)], axis=1)
        return jnp.dot(mixed, w_ref[...], preferred_element_type=F32)

    y = project(0)
    for k in range(n_chunks):
        y_next = project(k + 1) if k + 1 < n_chunks else None
        hn, un = _post_norms(y, h_ref[rows(k), :], gpost_ref[...], gffn_ref[...], None)
        ho_ref[rows(k), :] = hn
        u_ref[rows(k), :] = un
        y = y_next


def _outproj(att, ret, w, gpost, gffn, h):
    row = lambda i: (i, 0)
    const = lambda i: (0, 0)
    return pl.pallas_call(
        _outproj_kernel,
        grid=(M_ROWS // TM_OUT,),
        in_specs=[
            pl.BlockSpec((ATT_WIDTH // SLAB, TM_OUT, SLAB), lambda i: (0, i, 0)),
            pl.BlockSpec((RET_WIDTH // SLAB, TM_OUT, SLAB), lambda i: (0, i, 0)),
            pl.BlockSpec((D_MODEL, D_MODEL), const),
            pl.BlockSpec((1, D_MODEL), const),
            pl.BlockSpec((1, D_MODEL), const),
            pl.BlockSpec((TM_OUT, D_MODEL), row),
        ],
        out_specs=[
            pl.BlockSpec((TM_OUT, D_MODEL), row),
            pl.BlockSpec((TM_OUT, D_MODEL), row),
        ],
        out_shape=[
            jax.ShapeDtypeStruct((M_ROWS, D_MODEL), F32),
            jax.ShapeDtypeStruct((M_ROWS, D_MODEL), BF16),
        ],
        input_output_aliases={5: 0},
        compiler_params=pltpu.CompilerParams(
            dimension_semantics=("arbitrary",),
            vmem_limit_bytes=VMEM_LIMIT),
        name="outproj",
    )(att, ret, w, gpost, gffn, h)


def _ffn_steps(u_ref, wg_ref, wu_ref, wd_ref, acc_ref, finish, side_job=lambda: None):
    j = pl.program_id(1)
    last = pl.num_programs(1) - 1

    def swiglu(rows):
        u = u_ref[rows, :]
        g = jnp.dot(u, wg_ref[...], preferred_element_type=F32)
        up = jnp.dot(u, wu_ref[...], preferred_element_type=F32)
        f = (g * (1.0 / (1.0 + jnp.exp(-g))) * up).astype(BF16)
        return jnp.dot(f, wd_ref[...], preferred_element_type=F32)

    @pl.when(j == 0)
    def _():
        side_job()
        acc_ref[...] = swiglu(slice(None))

    @pl.when((j > 0) & (j < last))
    def _():
        side_job()
        acc_ref[...] += swiglu(slice(None))

    @pl.when(j == last)
    def _():
        side_job()
        rows = _chunk_slices(FFN_CHUNKS)
        y_next = acc_ref[rows[0], :] + swiglu(rows[0])
        for k, r in enumerate(rows):
            y = y_next
            if k + 1 < len(rows):
                y_next = acc_ref[rows[k + 1], :] + swiglu(rows[k + 1])
            finish(r, y)


def _ffn_mid_kernel(layer_ref, u_ref, wg_ref, wu_ref, wd_ref, gpost_ref, gnext_ref, h_ref, win_ref,
                    ho_ref, un_ref, winb_ref):
    del layer_ref
    i = pl.program_id(0)

    def cast_next_w_in_tile():
        winb_ref[...] = win_ref[...].astype(BF16)

    def finish(r, y):
        valid = _valid_rows(i * TM_FFN + r.start, r.stop - r.start)
        hn, un = _post_norms(y, h_ref[r, :], gpost_ref[...], gnext_ref[...], valid)
        ho_ref[r, :] = hn
        un_ref[r, :] = un

    _ffn_steps(u_ref, wg_ref, wu_ref, wd_ref, ho_ref, finish, cast_next_w_in_tile)


def _ffn_mid(u, wg, wu, wd, gpost, gnext, h, w_in, next_layer):
    row = lambda i, j, l: (i, 0)
    const = lambda i, j, l: (0, 0)
    n_i = M_ROWS // TM_FFN
    n_j = D_FF // TF_FFN
    assert n_i >= IN_CAST_TILES and n_j * TN_IN == IN_COLS
    ct = D_MODEL // IN_CAST_TILES
    tile = lambda i: jnp.minimum(i, IN_CAST_TILES - 1)
    return pl.pallas_call(
        _ffn_mid_kernel,
        grid_spec=pltpu.PrefetchScalarGridSpec(
            num_scalar_prefetch=1,
            grid=(n_i, n_j),
            in_specs=[
                pl.BlockSpec((TM_FFN, D_MODEL), row),
                pl.BlockSpec((None, D_MODEL, TF_FFN), lambda i, j, l: (j, 0, 0)),
                pl.BlockSpec((None, D_MODEL, TF_FFN), lambda i, j, l: (j, 0, 0)),
                pl.BlockSpec((TF_FFN, D_MODEL), lambda i, j, l: (j, 0)),
                pl.BlockSpec((1, D_MODEL), const),
                pl.BlockSpec((1, D_MODEL), const),
                pl.BlockSpec((TM_FFN, D_MODEL), row),
                pl.BlockSpec((None, ct, TN_IN), lambda i, j, l: (l[0], tile(i), j)),
            ],
            out_specs=[
                pl.BlockSpec((TM_FFN, D_MODEL), row),
                pl.BlockSpec((TM_FFN, D_MODEL), row),
                pl.BlockSpec((None, ct, TN_IN), lambda i, j, l: (j, i, 0)),
            ],
        ),
        out_shape=[
            jax.ShapeDtypeStruct((M_ROWS, D_MODEL), F32),
            jax.ShapeDtypeStruct((M_ROWS, D_MODEL), BF16),
            jax.ShapeDtypeStruct((n_j, n_i * ct, TN_IN), BF16),
        ],
        input_output_aliases={7: 0},
        compiler_params=pltpu.CompilerParams(
            dimension_semantics=("arbitrary", "arbitrary"),
            vmem_limit_bytes=VMEM_LIMIT),
        name="ffn",
    )(next_layer, u, wg, wu, wd, gpost, gnext, h, w_in)


def _token_block_copies(tile, acc_ref, slot, out_ref, sem):
    copies = []
    for q in range(TM_FFN // BLOCK):
        gb = tile * (TM_FFN // BLOCK) + q
        b = sum((gb >= k * NB).astype(jnp.int32) for k in range(1, BATCH))
        n = gb - b * NB
        dst = pl.multiple_of((b * (SEQ // BLOCK) + jnp.maximum(n - 1, 0)) * BLOCK, BLOCK)
        copy = pltpu.make_async_copy(
            acc_ref.at[slot, pl.ds(q * BLOCK, BLOCK), :], out_ref.at[pl.ds(dst, BLOCK), :],
            sem.at[slot])
        copies.append((n >= 1, copy))
    return copies


def _ffn_last_kernel(u_ref, wg_ref, wu_ref, wd_ref, gpost_ref, h_ref, out_ref, acc_ref, sem):
    i = pl.program_id(0)
    j = pl.program_id(1)
    slot = i % 2
    acc = acc_ref.at[slot]

    def finish(r, y):
        acc[r, :] = h_ref[r, :] + _rms(y, gpost_ref[...])

    _ffn_steps(u_ref, wg_ref, wu_ref, wd_ref, acc, finish)

    @pl.when(j == pl.num_programs(1) - 1)
    def _():
        for is_token, copy in _token_block_copies(i, acc_ref, slot, out_ref, sem):
            @pl.when(is_token)
            def _():
                copy.start()

        @pl.when(i > 0)
        def _():
            for is_token, copy in _token_block_copies(i - 1, acc_ref, 1 - slot, out_ref, sem):
                @pl.when(is_token)
                def _():
                    copy.wait()

        @pl.when(i == pl.num_programs(0) - 1)
        def _():
            for is_token, copy in _token_block_copies(i, acc_ref, slot, out_ref, sem):
                @pl.when(is_token)
                def _():
                    copy.wait()


def _ffn_last(u, wg, wu, wd, gpost, h):
    row = lambda i, j: (i, 0)
    const = lambda i, j: (0, 0)
    return pl.pallas_call(
        _ffn_last_kernel,
        grid=(M_ROWS // TM_FFN, D_FF // TF_FFN),
        in_specs=[
            pl.BlockSpec((TM_FFN, D_MODEL), row),
            pl.BlockSpec((None, D_MODEL, TF_FFN), lambda i, j: (j, 0, 0)),
            pl.BlockSpec((None, D_MODEL, TF_FFN), lambda i, j: (j, 0, 0)),
            pl.BlockSpec((TF_FFN, D_MODEL), lambda i, j: (j, 0)),
            pl.BlockSpec((1, D_MODEL), const),
            pl.BlockSpec((TM_FFN, D_MODEL), row),
        ],
        out_specs=pl.BlockSpec(memory_space=pl.ANY),
        out_shape=jax.ShapeDtypeStruct((BATCH * SEQ, D_MODEL), F32),
        scratch_shapes=[
            pltpu.VMEM((2, TM_FFN, D_MODEL), F32),
            pltpu.SemaphoreType.DMA((2,)),
        ],
        compiler_params=pltpu.CompilerParams(
            dimension_semantics=("arbitrary", "arbitrary"),
            vmem_limit_bytes=VMEM_LIMIT),
        name="ffn_last",
    )(u, wg, wu, wd, gpost, h)


def _rope_tables():
    pos = (jnp.arange(NP) - PAD_FRONT).astype(F32)
    half_a = ROT_DIM // 2
    inv_a = ROPE_THETA ** (-jnp.arange(half_a, dtype=F32) / half_a)
    ang_a = pos[:, None] * inv_a[None, :]
    cos_a, sin_a = jnp.cos(ang_a), jnp.sin(ang_a)
    rest = ATT_HEAD_DIM - ROT_DIM
    ca = jnp.concatenate([cos_a, cos_a, jnp.ones((NP, rest), F32)], axis=1)
    s1 = jnp.concatenate([jnp.zeros((NP, half_a), F32), sin_a, jnp.zeros((NP, rest), F32)], axis=1)
    s2 = jnp.concatenate([-sin_a, jnp.zeros((NP, ATT_HEAD_DIM - half_a), F32)], axis=1)
    half_r = RET_HEAD_DIM // 2
    inv_r = RET_THETA ** (-jnp.arange(half_r, dtype=F32) / half_r)
    ang_r = pos[:, None] * inv_r[None, :]
    return ca, s1, s2, jnp.cos(ang_r), jnp.sin(ang_r)


def kernel(x, meta_tokens, w_in, w_out, attn_sink, ret_decay_fwd, ret_decay_bwd, ret_norm,
           norm_mix_pre, norm_mix_post, w_gate, w_up, w_down, norm_ffn_pre, norm_ffn_post):
    tabs = _rope_tables()
    row = lambda a, l: a[l].reshape(1, -1).astype(F32)
    w_in, w_out, w_gate, w_up, w_down = (
        w.astype(F32) for w in (w_in, w_out, w_gate, w_up, w_down))
    h, u, w_in_l = _embed(x, meta_tokens, row(norm_mix_pre, 0), w_in)
    for l in range(DEPTH):
        layer = jnp.full((1,), l, jnp.int32)
        proj, wg, wu, wd = _inproj(u, w_in_l, layer, tabs, row(ret_norm, l), w_gate, w_up, w_down)
        att = _attention(proj, attn_sink[l].astype(F32))
        dec = jnp.stack([ret_decay_fwd[l], ret_decay_bwd[l]]).astype(F32)
        ret, w_out_l = _retention(proj, dec, w_out, l)
        h, u = _outproj(att, ret, w_out_l, row(norm_mix_post, l), row(norm_ffn_pre, l), h)
        if l + 1 < DEPTH:
            h, u, w_in_l = _ffn_mid(u, wg, wu, wd, row(norm_ffn_post, l), row(norm_mix_pre, l + 1),
                                    h, w_in, jnp.full((1,), l + 1, jnp.int32))
        else:
            out = _ffn_last(u, wg, wu, wd, row(norm_ffn_post, l), h)
    return out.reshape(BATCH, SEQ, D_MODEL)
```

```python
import numpy as np

import jax
import jax.numpy as jnp
from jax import lax
from jax.experimental import pallas as pl
from jax.experimental.pallas import tpu as pltpu

F32 = jnp.float32
BF16 = jnp.bfloat16

D_MODEL = 2048
BATCH = 4
SEQ = 4096
DEPTH = 4
N_META = 16
BLOCK = 128
SLAB = 128
WINDOW = 128
PAD_FRONT = BLOCK - N_META
NP = PAD_FRONT + N_META + SEQ
NB = NP // BLOCK
M_ROWS = BATCH * NP
ATT_HEAD_DIM = 128
ATT_WIDTH = 1024
ATT_HEADS = 8
ATT_KV_HEADS = 2
ATT_GROUP = ATT_HEADS // ATT_KV_HEADS
KV_WIDTH = ATT_KV_HEADS * ATT_HEAD_DIM
ROT_DIM = 32
ROPE_THETA = 500000.0
RET_WIDTH = 1024
RET_HEAD_DIM = 256
RET_HEADS = 4
RET_THETA = 10000.0
D_FF = 5632
IN_COLS = ATT_WIDTH + 2 * KV_WIDTH + 4 * RET_WIDTH
EPS = 1e-6
NEG = -1e30
LOG2E = 1.4426950408889634
ATT_QSCALE = ATT_HEAD_DIM ** -0.5 * LOG2E

COL_AQ = 0
COL_AK = ATT_WIDTH
COL_AV = COL_AK + KV_WIDTH
COL_RQ = COL_AV + KV_WIDTH
COL_RK = COL_RQ + RET_WIDTH
COL_RV = COL_RK + RET_WIDTH
COL_RG = COL_RV + RET_WIDTH

VMEM_LIMIT = 56 * 1024 * 1024

TM_IN = 1408
TN_IN = 512
TM_OUT = 768
TM_FFN = 768
TF_FFN = 512
IN_CAST_TILES = 16
CAST_TILES = 8
EMBED_BLOCKS = 3
RET_CHUNK = 256
RET_CHUNKS = (NP - BLOCK) // RET_CHUNK
IN_CHUNKS = (176,) * 8
OUT_CHUNKS = (192, 192, 192, 192)
FFN_CHUNKS = (256, 256, 256)
assert sum(IN_CHUNKS) == TM_IN and sum(OUT_CHUNKS) == TM_OUT and sum(FFN_CHUNKS) == TM_FFN


def _chunk_slices(sizes):
    bounds = [sum(sizes[:k]) for k in range(len(sizes) + 1)]
    return [slice(a, b) for a, b in zip(bounds[:-1], bounds[1:])]


def _rms(x, g):
    return x * lax.rsqrt(jnp.mean(x * x, axis=-1, keepdims=True) + EPS) * g


def _valid_rows(row0, rows):
    r = row0 + lax.broadcasted_iota(jnp.int32, (rows, 1), 0)
    ok = r >= 0
    for b in range(BATCH):
        ok = ok & ~((r >= b * NP) & (r < b * NP + PAD_FRONT))
    return ok


def _embed_kernel(*refs):
    x_refs = refs[:EMBED_BLOCKS]
    meta_ref, g_ref, w_ref, h_ref, u_ref, wb_ref = refs[EMBED_BLOCKS:]
    n = pl.program_id(1)
    wb_ref[...] = w_ref[...].astype(BF16)

    def copy(x_ref, rows):
        xv = x_ref[0]
        h_ref[rows, :] = xv
        u_ref[rows, :] = _rms(xv, g_ref[...]).astype(BF16)

    @pl.when(n == 0)
    def _():
        h_ref[0:PAD_FRONT, :] = jnp.zeros((PAD_FRONT, D_MODEL), F32)
        u_ref[0:PAD_FRONT, :] = jnp.zeros((PAD_FRONT, D_MODEL), BF16)
        m = meta_ref[...]
        h_ref[PAD_FRONT:BLOCK, :] = m
        u_ref[PAD_FRONT:BLOCK, :] = _rms(m, g_ref[...]).astype(BF16)

    @pl.when(n > 0)
    def _():
        copy(x_refs[0], slice(0, BLOCK))

    for t in range(1, EMBED_BLOCKS):
        copy(x_refs[t], slice(t * BLOCK, (t + 1) * BLOCK))


def _embed(x, meta, g, w_in):
    rows = EMBED_BLOCKS * BLOCK
    steps = NB // EMBED_BLOCKS
    wr, wc = D_MODEL // BATCH, IN_COLS // steps
    assert wr * BATCH == D_MODEL and wc * steps == IN_COLS and wc == TN_IN

    def x_spec(t):
        return pl.BlockSpec(
            (1, BLOCK, D_MODEL), lambda b, n: (b, jnp.maximum(n * EMBED_BLOCKS + t - 1, 0), 0))

    return pl.pallas_call(
        _embed_kernel,
        grid=(BATCH, steps),
        in_specs=[x_spec(t) for t in range(EMBED_BLOCKS)] + [
            pl.BlockSpec((N_META, D_MODEL), lambda b, n: (0, 0)),
            pl.BlockSpec((1, D_MODEL), lambda b, n: (0, 0)),
            pl.BlockSpec((None, wr, wc), lambda b, n: (0, b, n)),
        ],
        out_specs=[
            pl.BlockSpec((rows, D_MODEL), lambda b, n: (b * steps + n, 0)),
            pl.BlockSpec((rows, D_MODEL), lambda b, n: (b * steps + n, 0)),
            pl.BlockSpec((None, wr, wc), lambda b, n: (n, b, 0)),
        ],
        out_shape=[
            jax.ShapeDtypeStruct((M_ROWS, D_MODEL), F32),
            jax.ShapeDtypeStruct((M_ROWS, D_MODEL), BF16),
            jax.ShapeDtypeStruct((steps, D_MODEL, wc), BF16),
        ],
        compiler_params=pltpu.CompilerParams(
            dimension_semantics=("arbitrary", "arbitrary"),
            vmem_limit_bytes=VMEM_LIMIT),
        name="embed",
    )(*([x] * EMBED_BLOCKS), meta, g, w_in)


def _inproj_kernel(layer_ref, u_ref, w_ref, ca_ref, s1_ref, s2_ref, cr_ref, sr_ref, gain_ref,
                   wg_in, wu_in, wd_in, o_ref, wg_out, wu_out, wd_out):
    del layer_ref
    j = pl.program_id(1)
    tile_row0 = (pl.program_id(0) * TM_IN) % NP

    def tab_rows(r):
        start = tile_row0 + r.start
        start = jnp.where(start >= NP, start - NP, start)
        return pl.ds(pl.multiple_of(start, 8), r.stop - r.start)

    def rope_att(x, r):
        tr = tab_rows(r)
        return (x * ca_ref[tr, :]
                + pltpu.roll(x, ROT_DIM // 2, 1) * s1_ref[tr, :]
                + pltpu.roll(x, ATT_HEAD_DIM - ROT_DIM // 2, 1) * s2_ref[tr, :])

    def slab(acc, c):
        return acc[:, c * SLAB:(c + 1) * SLAB]

    def epi_att(n_heads, scale):
        def epi(acc, r):
            for c in range(TN_IN // SLAB):
                y = slab(acc, c)
                if c < n_heads:
                    y = rope_att(y, r)
                    if scale != 1.0:
                        y = y * scale
                o_ref[c, r, :] = y.astype(BF16)
        return epi

    def epi_ret(scale):
        def epi(acc, r):
            c = cr_ref[tab_rows(r), :]
            s = sr_ref[tab_rows(r), :]
            for hh in range(TN_IN // RET_HEAD_DIM):
                x1 = slab(acc, 2 * hh)
                x2 = slab(acc, 2 * hh + 1)
                y1 = x1 * c - x2 * s
                y2 = x2 * c + x1 * s
                if scale != 1.0:
                    y1 = y1 * scale
                    y2 = y2 * scale
                o_ref[2 * hh, r, :] = y1.astype(BF16)
                o_ref[2 * hh + 1, r, :] = y2.astype(BF16)
        return epi

    def epi_plain(acc, r):
        for c in range(TN_IN // SLAB):
            o_ref[c, r, :] = slab(acc, c).astype(BF16)

    def epi_gate(acc, r):
        y = acc * (1.0 / (1.0 + jnp.exp(-acc))) * gain_ref[...]
        for c in range(TN_IN // SLAB):
            o_ref[c, r, :] = slab(y, c).astype(BF16)

    def run(epi):
        wg_out[...] = wg_in[...].astype(BF16)
        wu_out[...] = wu_in[...].astype(BF16)
        wd_out[...] = wd_in[...].astype(BF16)
        for r in _chunk_slices(IN_CHUNKS):
            epi(jnp.dot(u_ref[r, :], w_ref[...], preferred_element_type=F32), r)

    n_aq = ATT_WIDTH // TN_IN
    j_akv = n_aq
    j_rq = COL_RQ // TN_IN
    j_rk = COL_RK // TN_IN
    j_rv = COL_RV // TN_IN
    j_rg = COL_RG // TN_IN

    @pl.when(j < n_aq)
    def _():
        run(epi_att(TN_IN // ATT_HEAD_DIM, ATT_QSCALE))

    @pl.when(j == j_akv)
    def _():
        run(epi_att(ATT_KV_HEADS, 1.0))

    @pl.when((j >= j_rq) & (j < j_rk))
    def _():
        run(epi_ret(1.0))

    @pl.when((j >= j_rk) & (j < j_rv))
    def _():
        run(epi_ret(RET_HEAD_DIM ** -0.5))

    @pl.when((j >= j_rv) & (j < j_rg))
    def _():
        run(epi_plain)

    @pl.when(j >= j_rg)
    def _():
        run(epi_gate)


def _inproj(u, w, layer, tabs, gain, w_gate, w_up, w_down):
    tab_spec = pl.BlockSpec((NP, 128), lambda i, j, l: (0, 0))
    n_i = M_ROWS // TM_IN
    n_j = IN_COLS // TN_IN
    assert n_i >= CAST_TILES and n_j * TF_FFN == D_FF
    ct = D_MODEL // CAST_TILES
    tile = lambda i: jnp.minimum(i, CAST_TILES - 1)
    return pl.pallas_call(
        _inproj_kernel,
        grid_spec=pltpu.PrefetchScalarGridSpec(
            num_scalar_prefetch=1,
            grid=(n_i, n_j),
            in_specs=[
                pl.BlockSpec((TM_IN, D_MODEL), lambda i, j, l: (i, 0)),
                pl.BlockSpec((None, D_MODEL, TN_IN), lambda i, j, l: (j, 0, 0)),
                tab_spec, tab_spec, tab_spec, tab_spec, tab_spec,
                pl.BlockSpec((1, TN_IN), lambda i, j, l: (0, jnp.maximum(j - COL_RG // TN_IN, 0))),
                pl.BlockSpec((None, ct, TF_FFN), lambda i, j, l: (l[0], tile(i), j)),
                pl.BlockSpec((None, ct, TF_FFN), lambda i, j, l: (l[0], tile(i), j)),
                pl.BlockSpec((None, TF_FFN, ct), lambda i, j, l: (l[0], j, tile(i))),
            ],
            out_specs=[
                pl.BlockSpec((TN_IN // SLAB, TM_IN, SLAB), lambda i, j, l: (j, i, 0)),
                pl.BlockSpec((None, ct, TF_FFN), lambda i, j, l: (j, i, 0)),
                pl.BlockSpec((None, ct, TF_FFN), lambda i, j, l: (j, i, 0)),
                pl.BlockSpec((TF_FFN, ct), lambda i, j, l: (j, i)),
            ],
        ),
        out_shape=[
            jax.ShapeDtypeStruct((IN_COLS // SLAB, M_ROWS, SLAB), BF16),
            jax.ShapeDtypeStruct((n_j, n_i * ct, TF_FFN), BF16),
            jax.ShapeDtypeStruct((n_j, n_i * ct, TF_FFN), BF16),
            jax.ShapeDtypeStruct((D_FF, n_i * ct), BF16),
        ],
        compiler_params=pltpu.CompilerParams(
            dimension_semantics=("arbitrary", "arbitrary"),
            vmem_limit_bytes=VMEM_LIMIT),
        name="inproj",
    )(layer, u, w, *tabs, gain, w_gate, w_up, w_down)


def _att_mask(n):
    row = lax.broadcasted_iota(jnp.int32, (BLOCK, 4 * BLOCK), 0)
    col = lax.broadcasted_iota(jnp.int32, (BLOCK, 4 * BLOCK), 1)
    qi = n * BLOCK + row
    kj = (n - 1) * BLOCK + col
    is_band = col < 3 * BLOCK
    band = is_band & (jnp.abs(qi - kj) <= WINDOW) & (kj >= PAD_FRONT) & (kj < NP)
    mj = col - 3 * BLOCK
    meta = (~is_band) & (mj >= PAD_FRONT) & (jnp.abs(qi - mj) > WINDOW)
    return band | meta


def _attn_kernel(sink_ref, q_ref, k_ref, v_ref, o_ref, bias_ref):
    kv = pl.program_id(1)
    rows = ATT_GROUP * BLOCK

    head = lax.broadcasted_iota(jnp.int32, (rows, 1), 0) // BLOCK
    sink = jnp.zeros((rows, 1), F32)
    for g in range(ATT_GROUP):
        sink = jnp.where(head == g, sink_ref[kv * ATT_GROUP + g] * LOG2E, sink)

    k_meta = k_ref[0:BLOCK, :]
    v_meta = v_ref[0:BLOCK, :]
    ones = jnp.ones((4 * BLOCK, ATT_HEAD_DIM), BF16)

    bias_ref[...] = jnp.where(_att_mask(2), 0.0, NEG).astype(F32)

    def block(n, bias, own_bias):
        r_own = pl.ds(pl.multiple_of(n * BLOCK, BLOCK), BLOCK)
        r_prev = pl.ds(pl.multiple_of(jnp.maximum(n - 1, 0) * BLOCK, BLOCK), BLOCK)
        r_next = pl.ds(pl.multiple_of(jnp.minimum(n + 1, NB - 1) * BLOCK, BLOCK), BLOCK)
        qs = jnp.concatenate([q_ref[g, r_own, :] for g in range(ATT_GROUP)], axis=0)
        ks = jnp.concatenate([k_ref[r_prev, :], k_ref[r_own, :], k_ref[r_next, :], k_meta], axis=0)
        vs = jnp.concatenate([v_ref[r_prev, :], v_ref[r_own, :], v_ref[r_next, :], v_meta], axis=0)
        vs = jnp.concatenate([vs, ones], axis=1)
        s = lax.dot_general(qs, ks, (((1,), (1,)), ((), ())), preferred_element_type=F32)
        bias4 = jnp.concatenate([bias] * ATT_GROUP, axis=0)
        if own_bias:
            s = s + bias4
        else:
            s = jnp.concatenate([s[:, :BLOCK] + bias4[:, :BLOCK], s[:, BLOCK:2 * BLOCK],
                                 s[:, 2 * BLOCK:] + bias4[:, 2 * BLOCK:]], axis=1)
        m = jnp.maximum(jnp.max(s, axis=-1, keepdims=True), sink)
        p = jnp.exp2(s - m).astype(BF16)
        o = jnp.dot(p, vs, preferred_element_type=F32)
        o = o[:, :ATT_HEAD_DIM] / (o[:, ATT_HEAD_DIM:] + jnp.exp2(sink - m))
        for g in range(ATT_GROUP):
            o_ref[g, r_own, :] = o[g * BLOCK:(g + 1) * BLOCK, :].astype(BF16)

    def edge(n):
        block(n, jnp.where(_att_mask(n), 0.0, NEG).astype(F32), True)

    edge(0)
    edge(1)

    def body(n, carry):
        block(n, bias_ref[...], False)
        return carry

    lax.fori_loop(2, NB - 1, body, 0, unroll=15)
    edge(NB - 1)


def _attention(proj, sink):
    assert ATT_HEAD_DIM == SLAB
    return pl.pallas_call(
        _attn_kernel,
        grid=(BATCH, ATT_KV_HEADS),
        in_specs=[
            pl.BlockSpec(memory_space=pltpu.SMEM),
            pl.BlockSpec((ATT_GROUP, NP, SLAB), lambda b, k: (k, b, 0)),
            pl.BlockSpec((None, NP, SLAB), lambda b, k: (COL_AK // SLAB + k, b, 0)),
            pl.BlockSpec((None, NP, SLAB), lambda b, k: (COL_AV // SLAB + k, b, 0)),
        ],
        out_specs=pl.BlockSpec((ATT_GROUP, NP, SLAB), lambda b, k: (k, b, 0)),
        out_shape=jax.ShapeDtypeStruct((ATT_HEADS, M_ROWS, SLAB), BF16),
        scratch_shapes=[pltpu.VMEM((BLOCK, 4 * BLOCK), F32)],
        compiler_params=pltpu.CompilerParams(
            dimension_semantics=("arbitrary", "arbitrary"),
            vmem_limit_bytes=VMEM_LIMIT),
        name="attention",
    )(sink, proj, proj, proj)


def _ret_kernel(dec_ref, q_ref, k_ref, v_ref, g_ref, wo_ref,
                o_ref, wob_ref, st_ref, sf_ref, sb_ref):
    wob_ref[...] = wo_ref[...].astype(BF16)
    hd = pl.program_id(1)
    lgf = -jnp.exp(jnp.full((1, RET_HEAD_DIM), dec_ref[0, hd], F32))
    lgb = -jnp.exp(jnp.full((1, RET_HEAD_DIM), dec_ref[1, hd], F32))

    def weights(L):
        idx = lax.broadcasted_iota(jnp.int32, (L, RET_HEAD_DIM), 0).astype(F32)
        r = lax.broadcasted_iota(jnp.int32, (L, L), 0)
        c = lax.broadcasted_iota(jnp.int32, (L, L), 1)
        diff = (r - c).astype(F32)
        return dict(
            xi_f=jnp.exp(lgf * (idx + 1.0)).astype(BF16),
            zeta_f=jnp.exp(lgf * (L - 1.0 - idx)).astype(BF16),
            xi_b=jnp.exp(lgb * (L - idx)).astype(BF16),
            zeta_b=jnp.exp(lgb * idx).astype(BF16),
            dec_f=jnp.exp(lgf * float(L)),
            dec_b=jnp.exp(lgb * float(L)),
            dmask=jnp.where(diff >= 0.0,
                            jnp.exp(lgf[:, :L] * jnp.maximum(diff, 0.0)),
                            jnp.exp(lgb[:, :L] * jnp.maximum(-diff, 0.0))))

    w_head = weights(BLOCK)
    w_main = weights(RET_CHUNK)

    def main_rows(c):
        return pl.ds(pl.multiple_of(BLOCK + (c - 1) * RET_CHUNK, BLOCK), RET_CHUNK)

    head_rows = slice(0, BLOCK)

    def rd(ref, rows):
        return jnp.concatenate([ref[0, rows, :], ref[1, rows, :]], axis=1)

    def kv_outer(rows, zeta):
        return lax.dot_general(rd(k_ref, rows) * zeta, rd(v_ref, rows), (((0,), (0,)), ((), ())),
                               preferred_element_type=F32)

    zeros = jnp.zeros((RET_HEAD_DIM, RET_HEAD_DIM), F32)
    st_ref[0, 0:RET_HEAD_DIM, :] = zeros.astype(BF16)
    sf_ref[...] = kv_outer(head_rows, w_head["zeta_f"])
    sb_ref[...] = zeros

    def scan(t, carry):
        cf = 1 + t
        cb = RET_CHUNKS - t
        st_ref[cf, 0:RET_HEAD_DIM, :] = sf_ref[...].astype(BF16)
        st_ref[cb, RET_HEAD_DIM:, :] = sb_ref[...].astype(BF16)
        sf_ref[...] = sf_ref[...] * w_main["dec_f"] + kv_outer(main_rows(cf), w_main["zeta_f"])
        sb_ref[...] = sb_ref[...] * w_main["dec_b"] + kv_outer(main_rows(cb), w_main["zeta_b"])
        return carry

    lax.fori_loop(0, RET_CHUNKS, scan, 0, unroll=16)
    st_ref[0, RET_HEAD_DIM:, :] = sb_ref[...].astype(BF16)

    def out(rows, c, w):
        q = rd(q_ref, rows)
        k = rd(k_ref, rows)
        v = rd(v_ref, rows)
        a = lax.dot_general(q, k, (((1,), (1,)), ((), ())), preferred_element_type=F32)
        p = (a * w["dmask"]).astype(BF16)
        qc = jnp.concatenate([q * w["xi_f"], q * w["xi_b"]], axis=1)
        y = (jnp.dot(p, v, preferred_element_type=F32)
             + jnp.dot(qc, st_ref[c], preferred_element_type=F32))
        y = y * lax.rsqrt(jnp.mean(y * y, axis=-1, keepdims=True) + EPS)
        for c in range(RET_HEAD_DIM // SLAB):
            yc = y[:, c * SLAB:(c + 1) * SLAB] * g_ref[c, rows, :].astype(F32)
            o_ref[c, rows, :] = yc.astype(BF16)

    out(head_rows, 0, w_head)

    def out_main(t, carry):
        c = 1 + t
        out(main_rows(c), c, w_main)
        return carry

    lax.fori_loop(0, RET_CHUNKS, out_main, 0, unroll=16)


def _retention(proj, dec, w_out, layer):
    head_slabs = RET_HEAD_DIM // SLAB

    def col_spec(col0):
        return pl.BlockSpec((head_slabs, NP, SLAB), lambda b, h: (col0 // RET_HEAD_DIM + h, b, 0))

    wr = D_MODEL // (BATCH * RET_HEADS)
    return pl.pallas_call(
        _ret_kernel,
        grid=(BATCH, RET_HEADS),
        in_specs=[
            pl.BlockSpec(memory_space=pltpu.SMEM),
            col_spec(COL_RQ), col_spec(COL_RK), col_spec(COL_RV), col_spec(COL_RG),
            pl.BlockSpec((None, wr, D_MODEL), lambda b, h: (layer, b * RET_HEADS + h, 0)),
        ],
        out_specs=[
            pl.BlockSpec((head_slabs, NP, SLAB), lambda b, h: (h, b, 0)),
            pl.BlockSpec((wr, D_MODEL), lambda b, h: (b * RET_HEADS + h, 0)),
        ],
        out_shape=[
            jax.ShapeDtypeStruct((RET_WIDTH // SLAB, M_ROWS, SLAB), BF16),
            jax.ShapeDtypeStruct((D_MODEL, D_MODEL), BF16),
        ],
        scratch_shapes=[
            pltpu.VMEM((RET_CHUNKS + 1, 2 * RET_HEAD_DIM, RET_HEAD_DIM), BF16),
            pltpu.VMEM((RET_HEAD_DIM, RET_HEAD_DIM), F32),
            pltpu.VMEM((RET_HEAD_DIM, RET_HEAD_DIM), F32),
        ],
        compiler_params=pltpu.CompilerParams(
            dimension_semantics=("arbitrary", "arbitrary"),
            vmem_limit_bytes=VMEM_LIMIT),
        name="retention",
    )(dec, proj, proj, proj, proj, w_out)


def _post_norms(y, h_rows, gpost, gnext, valid):
    hn = h_rows + _rms(y, gpost)
    scale = lax.rsqrt(jnp.mean(hn * hn, axis=-1, keepdims=True) + EPS)
    if valid is not None:
        scale = jnp.where(valid, scale, 0.0)
    return hn, (hn * scale * gnext).astype(BF16)


def _outproj_kernel(att_ref, ret_ref, w_ref, gpost_ref, gffn_ref, h_ref, ho_ref, u_ref):
    chunks = _chunk_slices(OUT_CHUNKS)
    n_chunks = len(chunks)
    rows = lambda k: chunks[k]

    def project(k):
        r = rows(k)
        mixed = jnp.concatenate(
            [att_ref[c, r, :] for c in range(ATT_WIDTH // SLAB)]
            + [ret_ref[c, r, :] for c in range(RET_WIDTH // SLAB)], axis=1)
        return jnp.dot(mixed, w_ref[...], preferred_element_type=F32)

    y = project(0)
    for k in range(n_chunks):
        y_next = project(k + 1) if k + 1 < n_chunks else None
        hn, un = _post_norms(y, h_ref[rows(k), :], gpost_ref[...], gffn_ref[...], None)
        ho_ref[rows(k), :] = hn
        u_ref[rows(k), :] = un
        y = y_next


def _outproj(att, ret, w, gpost, gffn, h):
    row = lambda i: (i, 0)
    const = lambda i: (0, 0)
    return pl.pallas_call(
        _outproj_kernel,
        grid=(M_ROWS // TM_OUT,),
        in_specs=[
            pl.BlockSpec((ATT_WIDTH // SLAB, TM_OUT, SLAB), lambda i: (0, i, 0)),
            pl.BlockSpec((RET_WIDTH // SLAB, TM_OUT, SLAB), lambda i: (0, i, 0)),
            pl.BlockSpec((D_MODEL, D_MODEL), const),
            pl.BlockSpec((1, D_MODEL), const),
            pl.BlockSpec((1, D_MODEL), const),
            pl.BlockSpec((TM_OUT, D_MODEL), row),
        ],
        out_specs=[
            pl.BlockSpec((TM_OUT, D_MODEL), row),
            pl.BlockSpec((TM_OUT, D_MODEL), row),
        ],
        out_shape=[
            jax.ShapeDtypeStruct((M_ROWS, D_MODEL), F32),
            jax.ShapeDtypeStruct((M_ROWS, D_MODEL), BF16),
        ],
        input_output_aliases={5: 0},
        compiler_params=pltpu.CompilerParams(
            dimension_semantics=("arbitrary",),
            vmem_limit_bytes=VMEM_LIMIT),
        name="outproj",
    )(att, ret, w, gpost, gffn, h)


def _ffn_steps(u_ref, wg_ref, wu_ref, wd_ref, acc_ref, finish, side_job=lambda: None):
    j = pl.program_id(1)
    last = pl.num_programs(1) - 1

    def swiglu(rows):
        u = u_ref[rows, :]
        g = jnp.dot(u, wg_ref[...], preferred_element_type=F32)
        up = jnp.dot(u, wu_ref[...], preferred_element_type=F32)
        f = (g * (1.0 / (1.0 + jnp.exp(-g))) * up).astype(BF16)
        return jnp.dot(f, wd_ref[...], preferred_element_type=F32)

    @pl.when(j == 0)
    def _():
        side_job()
        acc_ref[...] = swiglu(slice(None))

    @pl.when((j > 0) & (j < last))
    def _():
        side_job()
        acc_ref[...] += swiglu(slice(None))

    @pl.when(j == last)
    def _():
        side_job()
        rows = _chunk_slices(FFN_CHUNKS)
        y_next = acc_ref[rows[0], :] + swiglu(rows[0])
        for k, r in enumerate(rows):
            y = y_next
            if k + 1 < len(rows):
                y_next = acc_ref[rows[k + 1], :] + swiglu(rows[k + 1])
            finish(r, y)


def _ffn_mid_kernel(layer_ref, u_ref, wg_ref, wu_ref, wd_ref, gpost_ref, gnext_ref, h_ref, win_ref,
                    ho_ref, un_ref, winb_ref):
    del layer_ref
    i = pl.program_id(0)

    def cast_next_w_in_tile():
        winb_ref[...] = win_ref[...].astype(BF16)

    def finish(r, y):
        valid = _valid_rows(i * TM_FFN + r.start, r.stop - r.start)
        hn, un = _post_norms(y, h_ref[r, :], gpost_ref[...], gnext_ref[...], valid)
        ho_ref[r, :] = hn
        un_ref[r, :] = un

    _ffn_steps(u_ref, wg_ref, wu_ref, wd_ref, ho_ref, finish, cast_next_w_in_tile)


def _ffn_mid(u, wg, wu, wd, gpost, gnext, h, w_in, next_layer):
    row = lambda i, j, l: (i, 0)
    const = lambda i, j, l: (0, 0)
    n_i = M_ROWS // TM_FFN
    n_j = D_FF // TF_FFN
    assert n_i >= IN_CAST_TILES and n_j * TN_IN == IN_COLS
    ct = D_MODEL // IN_CAST_TILES
    tile = lambda i: jnp.minimum(i, IN_CAST_TILES - 1)
    return pl.pallas_call(
        _ffn_mid_kernel,
        grid_spec=pltpu.PrefetchScalarGridSpec(
            num_scalar_prefetch=1,
            grid=(n_i, n_j),
            in_specs=[
                pl.BlockSpec((TM_FFN, D_MODEL), row),
                pl.BlockSpec((None, D_MODEL, TF_FFN), lambda i, j, l: (j, 0, 0)),
                pl.BlockSpec((None, D_MODEL, TF_FFN), lambda i, j, l: (j, 0, 0)),
                pl.BlockSpec((TF_FFN, D_MODEL), lambda i, j, l: (j, 0)),
                pl.BlockSpec((1, D_MODEL), const),
                pl.BlockSpec((1, D_MODEL), const),
                pl.BlockSpec((TM_FFN, D_MODEL), row),
                pl.BlockSpec((None, ct, TN_IN), lambda i, j, l: (l[0], tile(i), j)),
            ],
            out_specs=[
                pl.BlockSpec((TM_FFN, D_MODEL), row),
                pl.BlockSpec((TM_FFN, D_MODEL), row),
                pl.BlockSpec((None, ct, TN_IN), lambda i, j, l: (j, i, 0)),
            ],
        ),
        out_shape=[
            jax.ShapeDtypeStruct((M_ROWS, D_MODEL), F32),
            jax.ShapeDtypeStruct((M_ROWS, D_MODEL), BF16),
            jax.ShapeDtypeStruct((n_j, n_i * ct, TN_IN), BF16),
        ],
        input_output_aliases={7: 0},
        compiler_params=pltpu.CompilerParams(
            dimension_semantics=("arbitrary", "arbitrary"),
            vmem_limit_bytes=VMEM_LIMIT),
        name="ffn",
    )(next_layer, u, wg, wu, wd, gpost, gnext, h, w_in)


def _token_block_copies(tile, acc_ref, slot, out_ref, sem):
    copies = []
    for q in range(TM_FFN // BLOCK):
        gb = tile * (TM_FFN // BLOCK) + q
        b = sum((gb >= k * NB).astype(jnp.int32) for k in range(1, BATCH))
        n = gb - b * NB
        dst = pl.multiple_of((b * (SEQ // BLOCK) + jnp.maximum(n - 1, 0)) * BLOCK, BLOCK)
        copy = pltpu.make_async_copy(
            acc_ref.at[slot, pl.ds(q * BLOCK, BLOCK), :], out_ref.at[pl.ds(dst, BLOCK), :],
            sem.at[slot])
        copies.append((n >= 1, copy))
    return copies


def _ffn_last_kernel(u_ref, wg_ref, wu_ref, wd_ref, gpost_ref, h_ref, out_ref, acc_ref, sem):
    i = pl.program_id(0)
    j = pl.program_id(1)
    slot = i % 2
    acc = acc_ref.at[slot]

    def finish(r, y):
        acc[r, :] = h_ref[r, :] + _rms(y, gpost_ref[...])

    _ffn_steps(u_ref, wg_ref, wu_ref, wd_ref, acc, finish)

    @pl.when(j == pl.num_programs(1) - 1)
    def _():
        for is_token, copy in _token_block_copies(i, acc_ref, slot, out_ref, sem):
            @pl.when(is_token)
            def _():
                copy.start()

        @pl.when(i > 0)
        def _():
            for is_token, copy in _token_block_copies(i - 1, acc_ref, 1 - slot, out_ref, sem):
                @pl.when(is_token)
                def _():
                    copy.wait()

        @pl.when(i == pl.num_programs(0) - 1)
        def _():
            for is_token, copy in _token_block_copies(i, acc_ref, slot, out_ref, sem):
                @pl.when(is_token)
                def _():
                    copy.wait()


def _ffn_last(u, wg, wu, wd, gpost, h):
    row = lambda i, j: (i, 0)
    const = lambda i, j: (0, 0)
    return pl.pallas_call(
        _ffn_last_kernel,
        grid=(M_ROWS // TM_FFN, D_FF // TF_FFN),
        in_specs=[
            pl.BlockSpec((TM_FFN, D_MODEL), row),
            pl.BlockSpec((None, D_MODEL, TF_FFN), lambda i, j: (j, 0, 0)),
            pl.BlockSpec((None, D_MODEL, TF_FFN), lambda i, j: (j, 0, 0)),
            pl.BlockSpec((TF_FFN, D_MODEL), lambda i, j: (j, 0)),
            pl.BlockSpec((1, D_MODEL), const),
            pl.BlockSpec((TM_FFN, D_MODEL), row),
        ],
        out_specs=pl.BlockSpec(memory_space=pl.ANY),
        out_shape=jax.ShapeDtypeStruct((BATCH * SEQ, D_MODEL), F32),
        scratch_shapes=[
            pltpu.VMEM((2, TM_FFN, D_MODEL), F32),
            pltpu.SemaphoreType.DMA((2,)),
        ],
        compiler_params=pltpu.CompilerParams(
            dimension_semantics=("arbitrary", "arbitrary"),
            vmem_limit_bytes=VMEM_LIMIT),
        name="ffn_last",
    )(u, wg, wu, wd, gpost, h)


def _rope_tables():
    f32 = np.float32
    pos = (np.arange(NP) - PAD_FRONT).astype(f32)
    half_a = ROT_DIM // 2
    inv_a = f32(ROPE_THETA) ** (-np.arange(half_a, dtype=f32) / f32(half_a))
    ang_a = pos[:, None] * inv_a[None, :]
    cos_a, sin_a = np.cos(ang_a), np.sin(ang_a)
    rest = ATT_HEAD_DIM - ROT_DIM
    ca = np.concatenate([cos_a, cos_a, np.ones((NP, rest), f32)], axis=1)
    s1 = np.concatenate([np.zeros((NP, half_a), f32), sin_a, np.zeros((NP, rest), f32)], axis=1)
    s2 = np.concatenate([-sin_a, np.zeros((NP, ATT_HEAD_DIM - half_a), f32)], axis=1)
    half_r = RET_HEAD_DIM // 2
    inv_r = f32(RET_THETA) ** (-np.arange(half_r, dtype=f32) / f32(half_r))
    ang_r = pos[:, None] * inv_r[None, :]
    return tuple(np.asarray(t, f32) for t in (ca, s1, s2, np.cos(ang_r), np.sin(ang_r)))


def kernel(x, meta_tokens, w_in, w_out, attn_sink, ret_decay_fwd, ret_decay_bwd, ret_norm,
           norm_mix_pre, norm_mix_post, w_gate, w_up, w_down, norm_ffn_pre, norm_ffn_post):
    tabs = _rope_tables()
    row = lambda a, l: a[l].reshape(1, -1).astype(F32)
    w_in, w_out, w_gate, w_up, w_down = (
        w.astype(F32) for w in (w_in, w_out, w_gate, w_up, w_down))
    h, u, w_in_l = _embed(x, meta_tokens, row(norm_mix_pre, 0), w_in)
    for l in range(DEPTH):
        layer = jnp.full((1,), l, jnp.int32)
        proj, wg, wu, wd = _inproj(u, w_in_l, layer, tabs, row(ret_norm, l), w_gate, w_up, w_down)
        att = _attention(proj, attn_sink[l].astype(F32))
        dec = jnp.stack([ret_decay_fwd[l], ret_decay_bwd[l]]).astype(F32)
        ret, w_out_l = _retention(proj, dec, w_out, l)
        h, u = _outproj(att, ret, w_out_l, row(norm_mix_post, l), row(norm_ffn_pre, l), h)
        if l + 1 < DEPTH:
            h, u, w_in_l = _ffn_mid(u, wg, wu, wd, row(norm_ffn_post, l), row(norm_mix_pre, l + 1),
                                    h, w_in, jnp.full((1,), l + 1, jnp.int32))
        else:
            out = _ffn_last(u, wg, wu, wd, row(norm_ffn_post, l), h)
    return out.reshape(BATCH, SEQ, D_MODEL)
```

```python
import numpy as np

import jax
import jax.numpy as jnp
from jax import lax
from jax.experimental import pallas as pl
from jax.experimental.pallas import tpu as pltpu

F32 = jnp.float32
BF16 = jnp.bfloat16

D_MODEL = 2048
BATCH = 4
SEQ = 4096
DEPTH = 4
N_META = 16
BLOCK = 128
SLAB = 128
WINDOW = 128
PAD_FRONT = BLOCK - N_META
NP = PAD_FRONT + N_META + SEQ
NB = NP // BLOCK
M_ROWS = BATCH * NP
ATT_HEAD_DIM = 128
ATT_WIDTH = 1024
ATT_HEADS = 8
ATT_KV_HEADS = 2
ATT_GROUP = ATT_HEADS // ATT_KV_HEADS
KV_WIDTH = ATT_KV_HEADS * ATT_HEAD_DIM
ROT_DIM = 32
ROPE_THETA = 500000.0
RET_WIDTH = 1024
RET_HEAD_DIM = 256
RET_HEADS = 4
RET_THETA = 10000.0
D_FF = 5632
IN_COLS = ATT_WIDTH + 2 * KV_WIDTH + 4 * RET_WIDTH
EPS = 1e-6
NEG = -1e30
LOG2E = 1.4426950408889634
ATT_QSCALE = ATT_HEAD_DIM ** -0.5 * LOG2E

COL_AQ = 0
COL_AK = ATT_WIDTH
COL_AV = COL_AK + KV_WIDTH
COL_RQ = COL_AV + KV_WIDTH
COL_RK = COL_RQ + RET_WIDTH
COL_RV = COL_RK + RET_WIDTH
COL_RG = COL_RV + RET_WIDTH

VMEM_LIMIT = 56 * 1024 * 1024

TM_IN = 1408
TN_IN = 512
TM_OUT = 768
TM_FFN = 768
TF_FFN = 512
IN_CAST_TILES = 16
CAST_TILES = 8
EMBED_BLOCKS = 3
RET_CHUNK = 256
RET_CHUNKS = (NP - BLOCK) // RET_CHUNK
IN_CHUNKS = (176,) * 8
OUT_CHUNKS = (192, 192, 192, 192)
FFN_CHUNKS = (256, 256, 256)
assert sum(IN_CHUNKS) == TM_IN and sum(OUT_CHUNKS) == TM_OUT and sum(FFN_CHUNKS) == TM_FFN


def _chunk_slices(sizes):
    bounds = [sum(sizes[:k]) for k in range(len(sizes) + 1)]
    return [slice(a, b) for a, b in zip(bounds[:-1], bounds[1:])]


def _rms(x, g):
    return x * lax.rsqrt(jnp.mean(x * x, axis=-1, keepdims=True) + EPS) * g


def _valid_rows(row0, rows):
    r = row0 + lax.broadcasted_iota(jnp.int32, (rows, 1), 0)
    ok = r >= 0
    for b in range(BATCH):
        ok = ok & ~((r >= b * NP) & (r < b * NP + PAD_FRONT))
    return ok


def _embed_kernel(*refs):
    x_refs = refs[:EMBED_BLOCKS]
    meta_ref, g_ref, w_ref, h_ref, u_ref, wb_ref = refs[EMBED_BLOCKS:]
    n = pl.program_id(1)
    wb_ref[...] = w_ref[...].astype(BF16)

    def copy(x_ref, rows):
        xv = x_ref[0]
        h_ref[rows, :] = xv
        u_ref[rows, :] = _rms(xv, g_ref[...]).astype(BF16)

    @pl.when(n == 0)
    def _():
        h_ref[0:PAD_FRONT, :] = jnp.zeros((PAD_FRONT, D_MODEL), F32)
        u_ref[0:PAD_FRONT, :] = jnp.zeros((PAD_FRONT, D_MODEL), BF16)
        m = meta_ref[...]
        h_ref[PAD_FRONT:BLOCK, :] = m
        u_ref[PAD_FRONT:BLOCK, :] = _rms(m, g_ref[...]).astype(BF16)

    @pl.when(n > 0)
    def _():
        copy(x_refs[0], slice(0, BLOCK))

    for t in range(1, EMBED_BLOCKS):
        copy(x_refs[t], slice(t * BLOCK, (t + 1) * BLOCK))


def _embed(x, meta, g, w_in):
    rows = EMBED_BLOCKS * BLOCK
    steps = NB // EMBED_BLOCKS
    wr, wc = D_MODEL // BATCH, IN_COLS // steps
    assert wr * BATCH == D_MODEL and wc * steps == IN_COLS and wc == TN_IN

    def x_spec(t):
        return pl.BlockSpec(
            (1, BLOCK, D_MODEL), lambda b, n: (b, jnp.maximum(n * EMBED_BLOCKS + t - 1, 0), 0))

    return pl.pallas_call(
        _embed_kernel,
        grid=(BATCH, steps),
        in_specs=[x_spec(t) for t in range(EMBED_BLOCKS)] + [
            pl.BlockSpec((N_META, D_MODEL), lambda b, n: (0, 0)),
            pl.BlockSpec((1, D_MODEL), lambda b, n: (0, 0)),
            pl.BlockSpec((None, wr, wc), lambda b, n: (0, b, n)),
        ],
        out_specs=[
            pl.BlockSpec((rows, D_MODEL), lambda b, n: (b * steps + n, 0)),
            pl.BlockSpec((rows, D_MODEL), lambda b, n: (b * steps + n, 0)),
            pl.BlockSpec((None, wr, wc), lambda b, n: (n, b, 0)),
        ],
        out_shape=[
            jax.ShapeDtypeStruct((M_ROWS, D_MODEL), F32),
            jax.ShapeDtypeStruct((M_ROWS, D_MODEL), BF16),
            jax.ShapeDtypeStruct((steps, D_MODEL, wc), BF16),
        ],
        compiler_params=pltpu.CompilerParams(
            dimension_semantics=("arbitrary", "arbitrary"),
            vmem_limit_bytes=VMEM_LIMIT),
        name="embed",
    )(*([x] * EMBED_BLOCKS), meta, g, w_in)


def _inproj_kernel(layer_ref, u_ref, w_ref, ca_ref, s1_ref, s2_ref, cr_ref, sr_ref, gain_ref,
                   wg_in, wu_in, wd_in, o_ref, wg_out, wu_out, wd_out):
    del layer_ref
    j = pl.program_id(1)
    tile_row0 = (pl.program_id(0) * TM_IN) % NP

    def tab_rows(r):
        start = tile_row0 + r.start
        start = jnp.where(start >= NP, start - NP, start)
        return pl.ds(pl.multiple_of(start, 8), r.stop - r.start)

    def rope_att(x, r):
        tr = tab_rows(r)
        return (x * ca_ref[tr, :]
                + pltpu.roll(x, ROT_DIM // 2, 1) * s1_ref[tr, :]
                + pltpu.roll(x, ATT_HEAD_DIM - ROT_DIM // 2, 1) * s2_ref[tr, :])

    def slab(acc, c):
        return acc[:, c * SLAB:(c + 1) * SLAB]

    def epi_att(n_heads, scale):
        def epi(acc, r):
            for c in range(TN_IN // SLAB):
                y = slab(acc, c)
                if c < n_heads:
                    y = rope_att(y, r)
                    if scale != 1.0:
                        y = y * scale
                o_ref[c, r, :] = y.astype(BF16)
        return epi

    def epi_ret(scale):
        def epi(acc, r):
            c = cr_ref[tab_rows(r), :]
            s = sr_ref[tab_rows(r), :]
            for hh in range(TN_IN // RET_HEAD_DIM):
                x1 = slab(acc, 2 * hh)
                x2 = slab(acc, 2 * hh + 1)
                y1 = x1 * c - x2 * s
                y2 = x2 * c + x1 * s
                if scale != 1.0:
                    y1 = y1 * scale
                    y2 = y2 * scale
                o_ref[2 * hh, r, :] = y1.astype(BF16)
                o_ref[2 * hh + 1, r, :] = y2.astype(BF16)
        return epi

    def epi_plain(acc, r):
        for c in range(TN_IN // SLAB):
            o_ref[c, r, :] = slab(acc, c).astype(BF16)

    def epi_gate(acc, r):
        y = acc * (1.0 / (1.0 + jnp.exp(-acc))) * gain_ref[...]
        for c in range(TN_IN // SLAB):
            o_ref[c, r, :] = slab(y, c).astype(BF16)

    def run(epi):
        wg_out[...] = wg_in[...].astype(BF16)
        wu_out[...] = wu_in[...].astype(BF16)
        wd_out[...] = wd_in[...].astype(BF16)
        for r in _chunk_slices(IN_CHUNKS):
            epi(jnp.dot(u_ref[r, :], w_ref[...], preferred_element_type=F32), r)

    n_aq = ATT_WIDTH // TN_IN
    j_akv = n_aq
    j_rq = COL_RQ // TN_IN
    j_rk = COL_RK // TN_IN
    j_rv = COL_RV // TN_IN
    j_rg = COL_RG // TN_IN

    @pl.when(j < n_aq)
    def _():
        run(epi_att(TN_IN // ATT_HEAD_DIM, ATT_QSCALE))

    @pl.when(j == j_akv)
    def _():
        run(epi_att(ATT_KV_HEADS, 1.0))

    @pl.when((j >= j_rq) & (j < j_rk))
    def _():
        run(epi_ret(1.0))

    @pl.when((j >= j_rk) & (j < j_rv))
    def _():
        run(epi_ret(RET_HEAD_DIM ** -0.5))

    @pl.when((j >= j_rv) & (j < j_rg))
    def _():
        run(epi_plain)

    @pl.when(j >= j_rg)
    def _():
        run(epi_gate)


def _inproj(u, w, layer, tabs, gain, w_gate, w_up, w_down):
    tab_spec = pl.BlockSpec((NP, 128), lambda i, j, l: (0, 0))
    n_i = M_ROWS // TM_IN
    n_j = IN_COLS // TN_IN
    assert n_i >= CAST_TILES and n_j * TF_FFN == D_FF
    ct = D_MODEL // CAST_TILES
    tile = lambda i: jnp.minimum(i, CAST_TILES - 1)
    return pl.pallas_call(
        _inproj_kernel,
        grid_spec=pltpu.PrefetchScalarGridSpec(
            num_scalar_prefetch=1,
            grid=(n_i, n_j),
            in_specs=[
                pl.BlockSpec((TM_IN, D_MODEL), lambda i, j, l: (i, 0)),
                pl.BlockSpec((None, D_MODEL, TN_IN), lambda i, j, l: (j, 0, 0)),
                tab_spec, tab_spec, tab_spec, tab_spec, tab_spec,
                pl.BlockSpec((1, TN_IN), lambda i, j, l: (0, jnp.maximum(j - COL_RG // TN_IN, 0))),
                pl.BlockSpec((None, ct, TF_FFN), lambda i, j, l: (l[0], tile(i), j)),
                pl.BlockSpec((None, ct, TF_FFN), lambda i, j, l: (l[0], tile(i), j)),
                pl.BlockSpec((None, TF_FFN, ct), lambda i, j, l: (l[0], j, tile(i))),
            ],
            out_specs=[
                pl.BlockSpec((TN_IN // SLAB, TM_IN, SLAB), lambda i, j, l: (j, i, 0)),
                pl.BlockSpec((None, ct, TF_FFN), lambda i, j, l: (j, i, 0)),
                pl.BlockSpec((None, ct, TF_FFN), lambda i, j, l: (j, i, 0)),
                pl.BlockSpec((TF_FFN, ct), lambda i, j, l: (j, i)),
            ],
        ),
        out_shape=[
            jax.ShapeDtypeStruct((IN_COLS // SLAB, M_ROWS, SLAB), BF16),
            jax.ShapeDtypeStruct((n_j, n_i * ct, TF_FFN), BF16),
            jax.ShapeDtypeStruct((n_j, n_i * ct, TF_FFN), BF16),
            jax.ShapeDtypeStruct((D_FF, n_i * ct), BF16),
        ],
        compiler_params=pltpu.CompilerParams(
            dimension_semantics=("arbitrary", "arbitrary"),
            vmem_limit_bytes=VMEM_LIMIT),
        name="inproj",
    )(layer, u, w, *tabs, gain, w_gate, w_up, w_down)


def _att_mask(n):
    row = lax.broadcasted_iota(jnp.int32, (BLOCK, 4 * BLOCK), 0)
    col = lax.broadcasted_iota(jnp.int32, (BLOCK, 4 * BLOCK), 1)
    qi = n * BLOCK + row
    kj = (n - 1) * BLOCK + col
    is_band = col < 3 * BLOCK
    band = is_band & (jnp.abs(qi - kj) <= WINDOW) & (kj >= PAD_FRONT) & (kj < NP)
    mj = col - 3 * BLOCK
    meta = (~is_band) & (mj >= PAD_FRONT) & (jnp.abs(qi - mj) > WINDOW)
    return band | meta


def _attn_kernel(sink_ref, q_ref, k_ref, v_ref, o_ref, bias_ref):
    kv = pl.program_id(1)
    rows = ATT_GROUP * BLOCK

    head = lax.broadcasted_iota(jnp.int32, (rows, 1), 0) // BLOCK
    sink = jnp.zeros((rows, 1), F32)
    for g in range(ATT_GROUP):
        sink = jnp.where(head == g, sink_ref[kv * ATT_GROUP + g] * LOG2E, sink)

    k_meta = k_ref[0:BLOCK, :]
    v_meta = v_ref[0:BLOCK, :]
    ones = jnp.ones((4 * BLOCK, ATT_HEAD_DIM), BF16)

    bias_ref[...] = jnp.where(_att_mask(2), 0.0, NEG).astype(F32)

    def block(n, bias, own_bias):
        blk = lambda b: slice(b * BLOCK, (b + 1) * BLOCK)
        r_own, r_prev, r_next = blk(n), blk(max(n - 1, 0)), blk(min(n + 1, NB - 1))
        qs = jnp.concatenate([q_ref[g, r_own, :] for g in range(ATT_GROUP)], axis=0)
        ks = jnp.concatenate([k_ref[r_prev, :], k_ref[r_own, :], k_ref[r_next, :], k_meta], axis=0)
        vs = jnp.concatenate([v_ref[r_prev, :], v_ref[r_own, :], v_ref[r_next, :], v_meta], axis=0)
        vs = jnp.concatenate([vs, ones], axis=1)
        s = lax.dot_general(qs, ks, (((1,), (1,)), ((), ())), preferred_element_type=F32)
        bias4 = jnp.concatenate([bias] * ATT_GROUP, axis=0)
        if own_bias:
            s = s + bias4
        else:
            s = jnp.concatenate([s[:, :BLOCK] + bias4[:, :BLOCK], s[:, BLOCK:2 * BLOCK],
                                 s[:, 2 * BLOCK:] + bias4[:, 2 * BLOCK:]], axis=1)
        m = jnp.maximum(jnp.max(s, axis=-1, keepdims=True), sink)
        p = jnp.exp2(s - m).astype(BF16)
        o = jnp.dot(p, vs, preferred_element_type=F32)
        o = o[:, :ATT_HEAD_DIM] / (o[:, ATT_HEAD_DIM:] + jnp.exp2(sink - m))
        for g in range(ATT_GROUP):
            o_ref[g, r_own, :] = o[g * BLOCK:(g + 1) * BLOCK, :].astype(BF16)

    def edge(n):
        block(n, jnp.where(_att_mask(n), 0.0, NEG).astype(F32), True)

    edge(0)
    edge(1)
    for n in range(2, NB - 1):
        block(n, bias_ref[...], False)
    edge(NB - 1)


def _attention(proj, sink):
    assert ATT_HEAD_DIM == SLAB
    return pl.pallas_call(
        _attn_kernel,
        grid=(BATCH, ATT_KV_HEADS),
        in_specs=[
            pl.BlockSpec(memory_space=pltpu.SMEM),
            pl.BlockSpec((ATT_GROUP, NP, SLAB), lambda b, k: (k, b, 0)),
            pl.BlockSpec((None, NP, SLAB), lambda b, k: (COL_AK // SLAB + k, b, 0)),
            pl.BlockSpec((None, NP, SLAB), lambda b, k: (COL_AV // SLAB + k, b, 0)),
        ],
        out_specs=pl.BlockSpec((ATT_GROUP, NP, SLAB), lambda b, k: (k, b, 0)),
        out_shape=jax.ShapeDtypeStruct((ATT_HEADS, M_ROWS, SLAB), BF16),
        scratch_shapes=[pltpu.VMEM((BLOCK, 4 * BLOCK), F32)],
        compiler_params=pltpu.CompilerParams(
            dimension_semantics=("arbitrary", "arbitrary"),
            vmem_limit_bytes=VMEM_LIMIT),
        name="attention",
    )(sink, proj, proj, proj)


def _ret_kernel(dec_ref, q_ref, k_ref, v_ref, g_ref, wo_ref,
                o_ref, wob_ref, st_ref, sf_ref, sb_ref):
    wob_ref[...] = wo_ref[...].astype(BF16)
    hd = pl.program_id(1)
    lgf = -jnp.exp(jnp.full((1, RET_HEAD_DIM), dec_ref[0, hd], F32))
    lgb = -jnp.exp(jnp.full((1, RET_HEAD_DIM), dec_ref[1, hd], F32))

    def weights(L):
        idx = lax.broadcasted_iota(jnp.int32, (L, RET_HEAD_DIM), 0).astype(F32)
        r = lax.broadcasted_iota(jnp.int32, (L, L), 0)
        c = lax.broadcasted_iota(jnp.int32, (L, L), 1)
        diff = (r - c).astype(F32)
        return dict(
            xi_f=jnp.exp(lgf * (idx + 1.0)).astype(BF16),
            zeta_f=jnp.exp(lgf * (L - 1.0 - idx)).astype(BF16),
            xi_b=jnp.exp(lgb * (L - idx)).astype(BF16),
            zeta_b=jnp.exp(lgb * idx).astype(BF16),
            dec_f=jnp.exp(lgf * float(L)),
            dec_b=jnp.exp(lgb * float(L)),
            dmask=jnp.where(diff >= 0.0,
                            jnp.exp(lgf[:, :L] * jnp.maximum(diff, 0.0)),
                            jnp.exp(lgb[:, :L] * jnp.maximum(-diff, 0.0))))

    w_head = weights(BLOCK)
    w_main = weights(RET_CHUNK)

    def main_rows(c):
        return pl.ds(pl.multiple_of(BLOCK + (c - 1) * RET_CHUNK, BLOCK), RET_CHUNK)

    head_rows = slice(0, BLOCK)

    def rd(ref, rows):
        return jnp.concatenate([ref[0, rows, :], ref[1, rows, :]], axis=1)

    def kv_outer(rows, zeta):
        return lax.dot_general(rd(k_ref, rows) * zeta, rd(v_ref, rows), (((0,), (0,)), ((), ())),
                               preferred_element_type=F32)

    zeros = jnp.zeros((RET_HEAD_DIM, RET_HEAD_DIM), F32)
    st_ref[0, 0:RET_HEAD_DIM, :] = zeros.astype(BF16)
    sf_ref[...] = kv_outer(head_rows, w_head["zeta_f"])
    sb_ref[...] = zeros

    def scan(t, carry):
        cf = 1 + t
        cb = RET_CHUNKS - t
        st_ref[cf, 0:RET_HEAD_DIM, :] = sf_ref[...].astype(BF16)
        st_ref[cb, RET_HEAD_DIM:, :] = sb_ref[...].astype(BF16)
        sf_ref[...] = sf_ref[...] * w_main["dec_f"] + kv_outer(main_rows(cf), w_main["zeta_f"])
        sb_ref[...] = sb_ref[...] * w_main["dec_b"] + kv_outer(main_rows(cb), w_main["zeta_b"])
        return carry

    lax.fori_loop(0, RET_CHUNKS, scan, 0, unroll=16)
    st_ref[0, RET_HEAD_DIM:, :] = sb_ref[...].astype(BF16)

    def out(rows, c, w):
        q = rd(q_ref, rows)
        k = rd(k_ref, rows)
        v = rd(v_ref, rows)
        a = lax.dot_general(q, k, (((1,), (1,)), ((), ())), preferred_element_type=F32)
        p = (a * w["dmask"]).astype(BF16)
        qc = jnp.concatenate([q * w["xi_f"], q * w["xi_b"]], axis=1)
        y = (jnp.dot(p, v, preferred_element_type=F32)
             + jnp.dot(qc, st_ref[c], preferred_element_type=F32))
        y = y * lax.rsqrt(jnp.mean(y * y, axis=-1, keepdims=True) + EPS)
        for c in range(RET_HEAD_DIM // SLAB):
            yc = y[:, c * SLAB:(c + 1) * SLAB] * g_ref[c, rows, :].astype(F32)
            o_ref[c, rows, :] = yc.astype(BF16)

    out(head_rows, 0, w_head)

    def out_main(t, carry):
        c = 1 + t
        out(main_rows(c), c, w_main)
        return carry

    lax.fori_loop(0, RET_CHUNKS, out_main, 0, unroll=16)


def _retention(proj, dec, w_out, layer):
    head_slabs = RET_HEAD_DIM // SLAB

    def col_spec(col0):
        return pl.BlockSpec((head_slabs, NP, SLAB), lambda b, h: (col0 // RET_HEAD_DIM + h, b, 0))

    wr = D_MODEL // (BATCH * RET_HEADS)
    return pl.pallas_call(
        _ret_kernel,
        grid=(BATCH, RET_HEADS),
        in_specs=[
            pl.BlockSpec(memory_space=pltpu.SMEM),
            col_spec(COL_RQ), col_spec(COL_RK), col_spec(COL_RV), col_spec(COL_RG),
            pl.BlockSpec((None, wr, D_MODEL), lambda b, h: (layer, b * RET_HEADS + h, 0)),
        ],
        out_specs=[
            pl.BlockSpec((head_slabs, NP, SLAB), lambda b, h: (h, b, 0)),
            pl.BlockSpec((wr, D_MODEL), lambda b, h: (b * RET_HEADS + h, 0)),
        ],
        out_shape=[
            jax.ShapeDtypeStruct((RET_WIDTH // SLAB, M_ROWS, SLAB), BF16),
            jax.ShapeDtypeStruct((D_MODEL, D_MODEL), BF16),
        ],
        scratch_shapes=[
            pltpu.VMEM((RET_CHUNKS + 1, 2 * RET_HEAD_DIM, RET_HEAD_DIM), BF16),
            pltpu.VMEM((RET_HEAD_DIM, RET_HEAD_DIM), F32),
            pltpu.VMEM((RET_HEAD_DIM, RET_HEAD_DIM), F32),
        ],
        compiler_params=pltpu.CompilerParams(
            dimension_semantics=("arbitrary", "arbitrary"),
            vmem_limit_bytes=VMEM_LIMIT),
        name="retention",
    )(dec, proj, proj, proj, proj, w_out)


def _post_norms(y, h_rows, gpost, gnext, valid):
    hn = h_rows + _rms(y, gpost)
    scale = lax.rsqrt(jnp.mean(hn * hn, axis=-1, keepdims=True) + EPS)
    if valid is not None:
        scale = jnp.where(valid, scale, 0.0)
    return hn, (hn * scale * gnext).astype(BF16)


def _outproj_kernel(att_ref, ret_ref, w_ref, gpost_ref, gffn_ref, h_ref, ho_ref, u_ref):
    chunks = _chunk_slices(OUT_CHUNKS)
    n_chunks = len(chunks)
    rows = lambda k: chunks[k]

    def project(k):
        r = rows(k)
        mixed = jnp.concatenate(
            [att_ref[c, r, :] for c in range(ATT_WIDTH // SLAB)]
            + [ret_ref[c, r, :] for c in range(RET_WIDTH // SLAB)], axis=1)
        return jnp.dot(mixed, w_ref[...], preferred_element_type=F32)

    y = project(0)
    for k in range(n_chunks):
        y_next = project(k + 1) if k + 1 < n_chunks else None
        hn, un = _post_norms(y, h_ref[rows(k), :], gpost_ref[...], gffn_ref[...], None)
        ho_ref[rows(k), :] = hn
        u_ref[rows(k), :] = un
        y = y_next


def _outproj(att, ret, w, gpost, gffn, h):
    row = lambda i: (i, 0)
    const = lambda i: (0, 0)
    return pl.pallas_call(
        _outproj_kernel,
        grid=(M_ROWS // TM_OUT,),
        in_specs=[
            pl.BlockSpec((ATT_WIDTH // SLAB, TM_OUT, SLAB), lambda i: (0, i, 0)),
            pl.BlockSpec((RET_WIDTH // SLAB, TM_OUT, SLAB), lambda i: (0, i, 0)),
            pl.BlockSpec((D_MODEL, D_MODEL), const),
            pl.BlockSpec((1, D_MODEL), const),
            pl.BlockSpec((1, D_MODEL), const),
            pl.BlockSpec((TM_OUT, D_MODEL), row),
        ],
        out_specs=[
            pl.BlockSpec((TM_OUT, D_MODEL), row),
            pl.BlockSpec((TM_OUT, D_MODEL), row),
        ],
        out_shape=[
            jax.ShapeDtypeStruct((M_ROWS, D_MODEL), F32),
            jax.ShapeDtypeStruct((M_ROWS, D_MODEL), BF16),
        ],
        input_output_aliases={5: 0},
        compiler_params=pltpu.CompilerParams(
            dimension_semantics=("arbitrary",),
            vmem_limit_bytes=VMEM_LIMIT),
        name="outproj",
    )(att, ret, w, gpost, gffn, h)


def _ffn_steps(u_ref, wg_ref, wu_ref, wd_ref, acc_ref, finish, side_job=lambda: None):
    j = pl.program_id(1)
    last = pl.num_programs(1) - 1

    def swiglu(rows):
        u = u_ref[rows, :]
        g = jnp.dot(u, wg_ref[...], preferred_element_type=F32)
        up = jnp.dot(u, wu_ref[...], preferred_element_type=F32)
        f = (g * (1.0 / (1.0 + jnp.exp(-g))) * up).astype(BF16)
        return jnp.dot(f, wd_ref[...], preferred_element_type=F32)

    @pl.when(j == 0)
    def _():
        side_job()
        acc_ref[...] = swiglu(slice(None))

    @pl.when((j > 0) & (j < last))
    def _():
        side_job()
        acc_ref[...] += swiglu(slice(None))

    @pl.when(j == last)
    def _():
        side_job()
        rows = _chunk_slices(FFN_CHUNKS)
        y_next = acc_ref[rows[0], :] + swiglu(rows[0])
        for k, r in enumerate(rows):
            y = y_next
            if k + 1 < len(rows):
                y_next = acc_ref[rows[k + 1], :] + swiglu(rows[k + 1])
            finish(r, y)


def _ffn_mid_kernel(layer_ref, u_ref, wg_ref, wu_ref, wd_ref, gpost_ref, gnext_ref, h_ref, win_ref,
                    ho_ref, un_ref, winb_ref):
    del layer_ref
    i = pl.program_id(0)

    def cast_next_w_in_tile():
        winb_ref[...] = win_ref[...].astype(BF16)

    def finish(r, y):
        valid = _valid_rows(i * TM_FFN + r.start, r.stop - r.start)
        hn, un = _post_norms(y, h_ref[r, :], gpost_ref[...], gnext_ref[...], valid)
        ho_ref[r, :] = hn
        un_ref[r, :] = un

    _ffn_steps(u_ref, wg_ref, wu_ref, wd_ref, ho_ref, finish, cast_next_w_in_tile)


def _ffn_mid(u, wg, wu, wd, gpost, gnext, h, w_in, next_layer):
    row = lambda i, j, l: (i, 0)
    const = lambda i, j, l: (0, 0)
    n_i = M_ROWS // TM_FFN
    n_j = D_FF // TF_FFN
    assert n_i >= IN_CAST_TILES and n_j * TN_IN == IN_COLS
    ct = D_MODEL // IN_CAST_TILES
    tile = lambda i: jnp.minimum(i, IN_CAST_TILES - 1)
    return pl.pallas_call(
        _ffn_mid_kernel,
        grid_spec=pltpu.PrefetchScalarGridSpec(
            num_scalar_prefetch=1,
            grid=(n_i, n_j),
            in_specs=[
                pl.BlockSpec((TM_FFN, D_MODEL), row),
                pl.BlockSpec((None, D_MODEL, TF_FFN), lambda i, j, l: (j, 0, 0)),
                pl.BlockSpec((None, D_MODEL, TF_FFN), lambda i, j, l: (j, 0, 0)),
                pl.BlockSpec((TF_FFN, D_MODEL), lambda i, j, l: (j, 0)),
                pl.BlockSpec((1, D_MODEL), const),
                pl.BlockSpec((1, D_MODEL), const),
                pl.BlockSpec((TM_FFN, D_MODEL), row),
                pl.BlockSpec((None, ct, TN_IN), lambda i, j, l: (l[0], tile(i), j)),
            ],
            out_specs=[
                pl.BlockSpec((TM_FFN, D_MODEL), row),
                pl.BlockSpec((TM_FFN, D_MODEL), row),
                pl.BlockSpec((None, ct, TN_IN), lambda i, j, l: (j, i, 0)),
            ],
        ),
        out_shape=[
            jax.ShapeDtypeStruct((M_ROWS, D_MODEL), F32),
            jax.ShapeDtypeStruct((M_ROWS, D_MODEL), BF16),
            jax.ShapeDtypeStruct((n_j, n_i * ct, TN_IN), BF16),
        ],
        input_output_aliases={7: 0},
        compiler_params=pltpu.CompilerParams(
            dimension_semantics=("arbitrary", "arbitrary"),
            vmem_limit_bytes=VMEM_LIMIT),
        name="ffn",
    )(next_layer, u, wg, wu, wd, gpost, gnext, h, w_in)


def _token_block_copies(tile, acc_ref, slot, out_ref, sem):
    copies = []
    for q in range(TM_FFN // BLOCK):
        gb = tile * (TM_FFN // BLOCK) + q
        b = sum((gb >= k * NB).astype(jnp.int32) for k in range(1, BATCH))
        n = gb - b * NB
        dst = pl.multiple_of((b * (SEQ // BLOCK) + jnp.maximum(n - 1, 0)) * BLOCK, BLOCK)
        copy = pltpu.make_async_copy(
            acc_ref.at[slot, pl.ds(q * BLOCK, BLOCK), :], out_ref.at[pl.ds(dst, BLOCK), :],
            sem.at[slot])
        copies.append((n >= 1, copy))
    return copies


def _ffn_last_kernel(u_ref, wg_ref, wu_ref, wd_ref, gpost_ref, h_ref, out_ref, acc_ref, sem):
    i = pl.program_id(0)
    j = pl.program_id(1)
    slot = i % 2
    acc = acc_ref.at[slot]

    def finish(r, y):
        acc[r, :] = h_ref[r, :] + _rms(y, gpost_ref[...])

    _ffn_steps(u_ref, wg_ref, wu_ref, wd_ref, acc, finish)

    @pl.when(j == pl.num_programs(1) - 1)
    def _():
        for is_token, copy in _token_block_copies(i, acc_ref, slot, out_ref, sem):
            @pl.when(is_token)
            def _():
                copy.start()

        @pl.when(i > 0)
        def _():
            for is_token, copy in _token_block_copies(i - 1, acc_ref, 1 - slot, out_ref, sem):
                @pl.when(is_token)
                def _():
                    copy.wait()

        @pl.when(i == pl.num_programs(0) - 1)
        def _():
            for is_token, copy in _token_block_copies(i, acc_ref, slot, out_ref, sem):
                @pl.when(is_token)
                def _():
                    copy.wait()


def _ffn_last(u, wg, wu, wd, gpost, h):
    row = lambda i, j: (i, 0)
    const = lambda i, j: (0, 0)
    return pl.pallas_call(
        _ffn_last_kernel,
        grid=(M_ROWS // TM_FFN, D_FF // TF_FFN),
        in_specs=[
            pl.BlockSpec((TM_FFN, D_MODEL), row),
            pl.BlockSpec((None, D_MODEL, TF_FFN), lambda i, j: (j, 0, 0)),
            pl.BlockSpec((None, D_MODEL, TF_FFN), lambda i, j: (j, 0, 0)),
            pl.BlockSpec((TF_FFN, D_MODEL), lambda i, j: (j, 0)),
            pl.BlockSpec((1, D_MODEL), const),
            pl.BlockSpec((TM_FFN, D_MODEL), row),
        ],
        out_specs=pl.BlockSpec(memory_space=pl.ANY),
        out_shape=jax.ShapeDtypeStruct((BATCH * SEQ, D_MODEL), F32),
        scratch_shapes=[
            pltpu.VMEM((2, TM_FFN, D_MODEL), F32),
            pltpu.SemaphoreType.DMA((2,)),
        ],
        compiler_params=pltpu.CompilerParams(
            dimension_semantics=("arbitrary", "arbitrary"),
            vmem_limit_bytes=VMEM_LIMIT),
        name="ffn_last",
    )(u, wg, wu, wd, gpost, h)


def _rope_tables():
    f32 = np.float32
    pos = (np.arange(NP) - PAD_FRONT).astype(f32)
    half_a = ROT_DIM // 2
    inv_a = f32(ROPE_THETA) ** (-np.arange(half_a, dtype=f32) / f32(half_a))
    ang_a = pos[:, None] * inv_a[None, :]
    cos_a, sin_a = np.cos(ang_a), np.sin(ang_a)
    rest = ATT_HEAD_DIM - ROT_DIM
    ca = np.concatenate([cos_a, cos_a, np.ones((NP, rest), f32)], axis=1)
    s1 = np.concatenate([np.zeros((NP, half_a), f32), sin_a, np.zeros((NP, rest), f32)], axis=1)
    s2 = np.concatenate([-sin_a, np.zeros((NP, ATT_HEAD_DIM - half_a), f32)], axis=1)
    half_r = RET_HEAD_DIM // 2
    inv_r = f32(RET_THETA) ** (-np.arange(half_r, dtype=f32) / f32(half_r))
    ang_r = pos[:, None] * inv_r[None, :]
    return tuple(np.asarray(t, f32) for t in (ca, s1, s2, np.cos(ang_r), np.sin(ang_r)))


def kernel(x, meta_tokens, w_in, w_out, attn_sink, ret_decay_fwd, ret_decay_bwd, ret_norm,
           norm_mix_pre, norm_mix_post, w_gate, w_up, w_down, norm_ffn_pre, norm_ffn_post):
    tabs = _rope_tables()
    row = lambda a, l: a[l].reshape(1, -1).astype(F32)
    w_in, w_out, w_gate, w_up, w_down = (
        w.astype(F32) for w in (w_in, w_out, w_gate, w_up, w_down))
    h, u, w_in_l = _embed(x, meta_tokens, row(norm_mix_pre, 0), w_in)
    for l in range(DEPTH):
        layer = jnp.full((1,), l, jnp.int32)
        proj, wg, wu, wd = _inproj(u, w_in_l, layer, tabs, row(ret_norm, l), w_gate, w_up, w_down)
        att = _attention(proj, attn_sink[l].astype(F32))
        dec = jnp.stack([ret_decay_fwd[l], ret_decay_bwd[l]]).astype(F32)
        ret, w_out_l = _retention(proj, dec, w_out, l)
        h, u = _outproj(att, ret, w_out_l, row(norm_mix_post, l), row(norm_ffn_pre, l), h)
        if l + 1 < DEPTH:
            h, u, w_in_l = _ffn_mid(u, wg, wu, wd, row(norm_ffn_post, l), row(norm_mix_pre, l + 1),
                                    h, w_in, jnp.full((1,), l + 1, jnp.int32))
        else:
            out = _ffn_last(u, wg, wu, wd, row(norm_ffn_post, l), h)
    return out.reshape(BATCH, SEQ, D_MODEL)
```

```python
import numpy as np

import jax
import jax.numpy as jnp
from jax import lax
from jax.experimental import pallas as pl
from jax.experimental.pallas import tpu as pltpu

F32 = jnp.float32
BF16 = jnp.bfloat16

D_MODEL = 2048
BATCH = 4
SEQ = 4096
DEPTH = 4
N_META = 16
BLOCK = 128
SLAB = 128
WINDOW = 128
PAD_FRONT = BLOCK - N_META
NP = PAD_FRONT + N_META + SEQ
NB = NP // BLOCK
M_ROWS = BATCH * NP
ATT_HEAD_DIM = 128
ATT_WIDTH = 1024
ATT_HEADS = 8
ATT_KV_HEADS = 2
ATT_GROUP = ATT_HEADS // ATT_KV_HEADS
KV_WIDTH = ATT_KV_HEADS * ATT_HEAD_DIM
ROT_DIM = 32
ROPE_THETA = 500000.0
RET_WIDTH = 1024
RET_HEAD_DIM = 256
RET_HEADS = 4
RET_THETA = 10000.0
D_FF = 5632
IN_COLS = ATT_WIDTH + 2 * KV_WIDTH + 4 * RET_WIDTH
EPS = 1e-6
NEG = -1e30
LOG2E = 1.4426950408889634
ATT_QSCALE = ATT_HEAD_DIM ** -0.5 * LOG2E

COL_AQ = 0
COL_AK = ATT_WIDTH
COL_AV = COL_AK + KV_WIDTH
COL_RQ = COL_AV + KV_WIDTH
COL_RK = COL_RQ + RET_WIDTH
COL_RV = COL_RK + RET_WIDTH
COL_RG = COL_RV + RET_WIDTH

VMEM_LIMIT = 56 * 1024 * 1024

TM_IN = 1408
TN_IN = 512
TM_OUT = 768
TM_FFN = 768
TF_FFN = 512
IN_CAST_TILES = 16
CAST_TILES = 8
EMBED_BLOCKS = 3
RET_CHUNK = 256
RET_CHUNKS = (NP - BLOCK) // RET_CHUNK
IN_CHUNKS = (176,) * 8
OUT_CHUNKS = (192, 192, 192, 192)
FFN_CHUNKS = (256, 256, 256)
assert sum(IN_CHUNKS) == TM_IN and sum(OUT_CHUNKS) == TM_OUT and sum(FFN_CHUNKS) == TM_FFN


def _chunk_slices(sizes):
    bounds = [sum(sizes[:k]) for k in range(len(sizes) + 1)]
    return [slice(a, b) for a, b in zip(bounds[:-1], bounds[1:])]


def _silu(x):
    h = 0.5 * x
    return h * jnp.tanh(h) + h


def _rms(x, g):
    return x * lax.rsqrt(jnp.mean(x * x, axis=-1, keepdims=True) + EPS) * g


def _valid_rows(row0, rows):
    r = row0 + lax.broadcasted_iota(jnp.int32, (rows, 1), 0)
    ok = r >= 0
    for b in range(BATCH):
        ok = ok & ~((r >= b * NP) & (r < b * NP + PAD_FRONT))
    return ok


def _embed_kernel(*refs):
    x_refs = refs[:EMBED_BLOCKS]
    meta_ref, g_ref, w_ref, h_ref, u_ref, wb_ref = refs[EMBED_BLOCKS:]
    n = pl.program_id(1)
    wb_ref[...] = w_ref[...].astype(BF16)

    def copy(x_ref, rows):
        xv = x_ref[0]
        h_ref[rows, :] = xv
        u_ref[rows, :] = _rms(xv, g_ref[...]).astype(BF16)

    @pl.when(n == 0)
    def _():
        h_ref[0:PAD_FRONT, :] = jnp.zeros((PAD_FRONT, D_MODEL), F32)
        u_ref[0:PAD_FRONT, :] = jnp.zeros((PAD_FRONT, D_MODEL), BF16)
        m = meta_ref[...]
        h_ref[PAD_FRONT:BLOCK, :] = m
        u_ref[PAD_FRONT:BLOCK, :] = _rms(m, g_ref[...]).astype(BF16)

    @pl.when(n > 0)
    def _():
        copy(x_refs[0], slice(0, BLOCK))

    for t in range(1, EMBED_BLOCKS):
        copy(x_refs[t], slice(t * BLOCK, (t + 1) * BLOCK))


def _embed(x, meta, g, w_in):
    rows = EMBED_BLOCKS * BLOCK
    steps = NB // EMBED_BLOCKS
    wr, wc = D_MODEL // BATCH, IN_COLS // steps
    assert wr * BATCH == D_MODEL and wc * steps == IN_COLS and wc == TN_IN

    def x_spec(t):
        return pl.BlockSpec(
            (1, BLOCK, D_MODEL), lambda b, n: (b, jnp.maximum(n * EMBED_BLOCKS + t - 1, 0), 0))

    return pl.pallas_call(
        _embed_kernel,
        grid=(BATCH, steps),
        in_specs=[x_spec(t) for t in range(EMBED_BLOCKS)] + [
            pl.BlockSpec((N_META, D_MODEL), lambda b, n: (0, 0)),
            pl.BlockSpec((1, D_MODEL), lambda b, n: (0, 0)),
            pl.BlockSpec((None, wr, wc), lambda b, n: (0, b, n)),
        ],
        out_specs=[
            pl.BlockSpec((rows, D_MODEL), lambda b, n: (b * steps + n, 0)),
            pl.BlockSpec((rows, D_MODEL), lambda b, n: (b * steps + n, 0)),
            pl.BlockSpec((None, wr, wc), lambda b, n: (n, b, 0)),
        ],
        out_shape=[
            jax.ShapeDtypeStruct((M_ROWS, D_MODEL), F32),
            jax.ShapeDtypeStruct((M_ROWS, D_MODEL), BF16),
            jax.ShapeDtypeStruct((steps, D_MODEL, wc), BF16),
        ],
        compiler_params=pltpu.CompilerParams(
            dimension_semantics=("arbitrary", "arbitrary"),
            vmem_limit_bytes=VMEM_LIMIT),
        name="embed",
    )(*([x] * EMBED_BLOCKS), meta, g, w_in)


def _inproj_kernel(layer_ref, u_ref, w_ref, ca_ref, s1_ref, s2_ref, cr_ref, sr_ref, gain_ref,
                   wg_in, wu_in, wd_in, o_ref, wg_out, wu_out, wd_out):
    del layer_ref
    j = pl.program_id(1)
    tile_row0 = (pl.program_id(0) * TM_IN) % NP

    def tab_rows(r):
        start = tile_row0 + r.start
        start = jnp.where(start >= NP, start - NP, start)
        return pl.ds(pl.multiple_of(start, 8), r.stop - r.start)

    def rope_att(x, r):
        tr = tab_rows(r)
        return (x * ca_ref[tr, :]
                + pltpu.roll(x, ROT_DIM // 2, 1) * s1_ref[tr, :]
                + pltpu.roll(x, ATT_HEAD_DIM - ROT_DIM // 2, 1) * s2_ref[tr, :])

    def slab(acc, c):
        return acc[:, c * SLAB:(c + 1) * SLAB]

    def epi_att(n_heads, scale):
        def epi(acc, r):
            for c in range(TN_IN // SLAB):
                y = slab(acc, c)
                if c < n_heads:
                    y = rope_att(y, r)
                    if scale != 1.0:
                        y = y * scale
                o_ref[c, r, :] = y.astype(BF16)
        return epi

    def epi_ret(scale):
        def epi(acc, r):
            c = cr_ref[tab_rows(r), :]
            s = sr_ref[tab_rows(r), :]
            for hh in range(TN_IN // RET_HEAD_DIM):
                x1 = slab(acc, 2 * hh)
                x2 = slab(acc, 2 * hh + 1)
                y1 = x1 * c - x2 * s
                y2 = x2 * c + x1 * s
                if scale != 1.0:
                    y1 = y1 * scale
                    y2 = y2 * scale
                o_ref[2 * hh, r, :] = y1.astype(BF16)
                o_ref[2 * hh + 1, r, :] = y2.astype(BF16)
        return epi

    def epi_plain(acc, r):
        for c in range(TN_IN // SLAB):
            o_ref[c, r, :] = slab(acc, c).astype(BF16)

    def epi_gate(acc, r):
        y = _silu(acc) * gain_ref[...]
        for c in range(TN_IN // SLAB):
            o_ref[c, r, :] = slab(y, c).astype(BF16)

    def run(epi):
        wg_out[...] = wg_in[...].astype(BF16)
        wu_out[...] = wu_in[...].astype(BF16)
        wd_out[...] = wd_in[...].astype(BF16)
        for r in _chunk_slices(IN_CHUNKS):
            epi(jnp.dot(u_ref[r, :], w_ref[...], preferred_element_type=F32), r)

    n_aq = ATT_WIDTH // TN_IN
    j_akv = n_aq
    j_rq = COL_RQ // TN_IN
    j_rk = COL_RK // TN_IN
    j_rv = COL_RV // TN_IN
    j_rg = COL_RG // TN_IN

    @pl.when(j < n_aq)
    def _():
        run(epi_att(TN_IN // ATT_HEAD_DIM, ATT_QSCALE))

    @pl.when(j == j_akv)
    def _():
        run(epi_att(ATT_KV_HEADS, 1.0))

    @pl.when((j >= j_rq) & (j < j_rk))
    def _():
        run(epi_ret(1.0))

    @pl.when((j >= j_rk) & (j < j_rv))
    def _():
        run(epi_ret(RET_HEAD_DIM ** -0.5))

    @pl.when((j >= j_rv) & (j < j_rg))
    def _():
        run(epi_plain)

    @pl.when(j >= j_rg)
    def _():
        run(epi_gate)


def _inproj(u, w, layer, tabs, gain, w_gate, w_up, w_down):
    tab_spec = pl.BlockSpec((NP, 128), lambda i, j, l: (0, 0))
    n_i = M_ROWS // TM_IN
    n_j = IN_COLS // TN_IN
    assert n_i >= CAST_TILES and n_j * TF_FFN == D_FF
    ct = D_MODEL // CAST_TILES
    tile = lambda i: jnp.minimum(i, CAST_TILES - 1)
    return pl.pallas_call(
        _inproj_kernel,
        grid_spec=pltpu.PrefetchScalarGridSpec(
            num_scalar_prefetch=1,
            grid=(n_i, n_j),
            in_specs=[
                pl.BlockSpec((TM_IN, D_MODEL), lambda i, j, l: (i, 0)),
                pl.BlockSpec((None, D_MODEL, TN_IN), lambda i, j, l: (j, 0, 0)),
                tab_spec, tab_spec, tab_spec, tab_spec, tab_spec,
                pl.BlockSpec((1, TN_IN), lambda i, j, l: (0, jnp.maximum(j - COL_RG // TN_IN, 0))),
                pl.BlockSpec((None, ct, TF_FFN), lambda i, j, l: (l[0], tile(i), j)),
                pl.BlockSpec((None, ct, TF_FFN), lambda i, j, l: (l[0], tile(i), j)),
                pl.BlockSpec((None, TF_FFN, ct), lambda i, j, l: (l[0], j, tile(i))),
            ],
            out_specs=[
                pl.BlockSpec((TN_IN // SLAB, TM_IN, SLAB), lambda i, j, l: (j, i, 0)),
                pl.BlockSpec((None, ct, TF_FFN), lambda i, j, l: (j, i, 0)),
                pl.BlockSpec((None, ct, TF_FFN), lambda i, j, l: (j, i, 0)),
                pl.BlockSpec((TF_FFN, ct), lambda i, j, l: (j, i)),
            ],
        ),
        out_shape=[
            jax.ShapeDtypeStruct((IN_COLS // SLAB, M_ROWS, SLAB), BF16),
            jax.ShapeDtypeStruct((n_j, n_i * ct, TF_FFN), BF16),
            jax.ShapeDtypeStruct((n_j, n_i * ct, TF_FFN), BF16),
            jax.ShapeDtypeStruct((D_FF, n_i * ct), BF16),
        ],
        compiler_params=pltpu.CompilerParams(
            dimension_semantics=("arbitrary", "arbitrary"),
            vmem_limit_bytes=VMEM_LIMIT),
        name="inproj",
    )(layer, u, w, *tabs, gain, w_gate, w_up, w_down)


def _att_mask(n):
    row = lax.broadcasted_iota(jnp.int32, (BLOCK, 4 * BLOCK), 0)
    col = lax.broadcasted_iota(jnp.int32, (BLOCK, 4 * BLOCK), 1)
    qi = n * BLOCK + row
    kj = (n - 1) * BLOCK + col
    is_band = col < 3 * BLOCK
    band = is_band & (jnp.abs(qi - kj) <= WINDOW) & (kj >= PAD_FRONT) & (kj < NP)
    mj = col - 3 * BLOCK
    meta = (~is_band) & (mj >= PAD_FRONT) & (jnp.abs(qi - mj) > WINDOW)
    return band | meta


def _attn_kernel(sink_ref, q_ref, k_ref, v_ref, o_ref, bias_ref):
    kv = pl.program_id(1)
    rows = ATT_GROUP * BLOCK

    head = lax.broadcasted_iota(jnp.int32, (rows, 1), 0) // BLOCK
    sink = jnp.zeros((rows, 1), F32)
    for g in range(ATT_GROUP):
        sink = jnp.where(head == g, sink_ref[kv * ATT_GROUP + g] * LOG2E, sink)

    k_meta = k_ref[0:BLOCK, :]
    v_meta = v_ref[0:BLOCK, :]
    ones = jnp.ones((4 * BLOCK, ATT_HEAD_DIM), BF16)

    bias_ref[...] = jnp.where(_att_mask(2), 0.0, NEG).astype(F32)

    def block(n, bias, own_bias):
        blk = lambda b: slice(b * BLOCK, (b + 1) * BLOCK)
        r_own, r_prev, r_next = blk(n), blk(max(n - 1, 0)), blk(min(n + 1, NB - 1))
        qs = jnp.concatenate([q_ref[g, r_own, :] for g in range(ATT_GROUP)], axis=0)
        ks = jnp.concatenate([k_ref[r_prev, :], k_ref[r_own, :], k_ref[r_next, :], k_meta], axis=0)
        vs = jnp.concatenate([v_ref[r_prev, :], v_ref[r_own, :], v_ref[r_next, :], v_meta], axis=0)
        vs = jnp.concatenate([vs, ones], axis=1)
        s = lax.dot_general(qs, ks, (((1,), (1,)), ((), ())), preferred_element_type=F32)
        bias4 = jnp.concatenate([bias] * ATT_GROUP, axis=0)
        if own_bias:
            s = s + bias4
        else:
            s = jnp.concatenate([s[:, :BLOCK] + bias4[:, :BLOCK], s[:, BLOCK:2 * BLOCK],
                                 s[:, 2 * BLOCK:] + bias4[:, 2 * BLOCK:]], axis=1)
        m = jnp.maximum(jnp.max(s, axis=-1, keepdims=True), sink)
        p = jnp.exp2(s - m).astype(BF16)
        o = jnp.dot(p, vs, preferred_element_type=F32)
        o = o[:, :ATT_HEAD_DIM] / (o[:, ATT_HEAD_DIM:] + jnp.exp2(sink - m))
        for g in range(ATT_GROUP):
            o_ref[g, r_own, :] = o[g * BLOCK:(g + 1) * BLOCK, :].astype(BF16)

    def edge(n):
        block(n, jnp.where(_att_mask(n), 0.0, NEG).astype(F32), True)

    edge(0)
    edge(1)
    for n in range(2, NB - 1):
        block(n, bias_ref[...], False)
    edge(NB - 1)


def _attention(proj, sink):
    assert ATT_HEAD_DIM == SLAB
    return pl.pallas_call(
        _attn_kernel,
        grid=(BATCH, ATT_KV_HEADS),
        in_specs=[
            pl.BlockSpec(memory_space=pltpu.SMEM),
            pl.BlockSpec((ATT_GROUP, NP, SLAB), lambda b, k: (k, b, 0)),
            pl.BlockSpec((None, NP, SLAB), lambda b, k: (COL_AK // SLAB + k, b, 0)),
            pl.BlockSpec((None, NP, SLAB), lambda b, k: (COL_AV // SLAB + k, b, 0)),
        ],
        out_specs=pl.BlockSpec((ATT_GROUP, NP, SLAB), lambda b, k: (k, b, 0)),
        out_shape=jax.ShapeDtypeStruct((ATT_HEADS, M_ROWS, SLAB), BF16),
        scratch_shapes=[pltpu.VMEM((BLOCK, 4 * BLOCK), F32)],
        compiler_params=pltpu.CompilerParams(
            dimension_semantics=("arbitrary", "arbitrary"),
            vmem_limit_bytes=VMEM_LIMIT),
        name="attention",
    )(sink, proj, proj, proj)


def _ret_kernel(dec_ref, q_ref, k_ref, v_ref, g_ref, wo_ref,
                o_ref, wob_ref, st_ref, sf_ref, sb_ref):
    wob_ref[...] = wo_ref[...].astype(BF16)
    hd = pl.program_id(1)
    lgf = -jnp.exp(jnp.full((1, RET_HEAD_DIM), dec_ref[0, hd], F32))
    lgb = -jnp.exp(jnp.full((1, RET_HEAD_DIM), dec_ref[1, hd], F32))

    def weights(L):
        idx = lax.broadcasted_iota(jnp.int32, (L, RET_HEAD_DIM), 0).astype(F32)
        r = lax.broadcasted_iota(jnp.int32, (L, L), 0)
        c = lax.broadcasted_iota(jnp.int32, (L, L), 1)
        diff = (r - c).astype(F32)
        return dict(
            xi_f=jnp.exp(lgf * (idx + 1.0)).astype(BF16),
            zeta_f=jnp.exp(lgf * (L - 1.0 - idx)).astype(BF16),
            xi_b=jnp.exp(lgb * (L - idx)).astype(BF16),
            zeta_b=jnp.exp(lgb * idx).astype(BF16),
            dec_f=jnp.exp(lgf * float(L)),
            dec_b=jnp.exp(lgb * float(L)),
            dmask=jnp.where(diff >= 0.0,
                            jnp.exp(lgf[:, :L] * jnp.maximum(diff, 0.0)),
                            jnp.exp(lgb[:, :L] * jnp.maximum(-diff, 0.0))))

    w_head = weights(BLOCK)
    w_main = weights(RET_CHUNK)

    def main_rows(c):
        return pl.ds(pl.multiple_of(BLOCK + (c - 1) * RET_CHUNK, BLOCK), RET_CHUNK)

    head_rows = slice(0, BLOCK)

    def rd(ref, rows):
        return jnp.concatenate([ref[0, rows, :], ref[1, rows, :]], axis=1)

    def kv_outer(rows, zeta):
        return lax.dot_general(rd(k_ref, rows) * zeta, rd(v_ref, rows), (((0,), (0,)), ((), ())),
                               preferred_element_type=F32)

    zeros = jnp.zeros((RET_HEAD_DIM, RET_HEAD_DIM), F32)
    st_ref[0, 0:RET_HEAD_DIM, :] = zeros.astype(BF16)
    sf_ref[...] = kv_outer(head_rows, w_head["zeta_f"])
    sb_ref[...] = zeros

    def scan(t, carry):
        cf = 1 + t
        cb = RET_CHUNKS - t
        st_ref[cf, 0:RET_HEAD_DIM, :] = sf_ref[...].astype(BF16)
        st_ref[cb, RET_HEAD_DIM:, :] = sb_ref[...].astype(BF16)
        sf_ref[...] = sf_ref[...] * w_main["dec_f"] + kv_outer(main_rows(cf), w_main["zeta_f"])
        sb_ref[...] = sb_ref[...] * w_main["dec_b"] + kv_outer(main_rows(cb), w_main["zeta_b"])
        return carry

    lax.fori_loop(0, RET_CHUNKS, scan, 0, unroll=16)
    st_ref[0, RET_HEAD_DIM:, :] = sb_ref[...].astype(BF16)

    def out(rows, c, w):
        q = rd(q_ref, rows)
        k = rd(k_ref, rows)
        v = rd(v_ref, rows)
        a = lax.dot_general(q, k, (((1,), (1,)), ((), ())), preferred_element_type=F32)
        p = (a * w["dmask"]).astype(BF16)
        qc = jnp.concatenate([q * w["xi_f"], q * w["xi_b"]], axis=1)
        y = (jnp.dot(p, v, preferred_element_type=F32)
             + jnp.dot(qc, st_ref[c], preferred_element_type=F32))
        y = y * lax.rsqrt(jnp.mean(y * y, axis=-1, keepdims=True) + EPS)
        for c in range(RET_HEAD_DIM // SLAB):
            yc = y[:, c * SLAB:(c + 1) * SLAB] * g_ref[c, rows, :].astype(F32)
            o_ref[c, rows, :] = yc.astype(BF16)

    out(head_rows, 0, w_head)

    def out_main(t, carry):
        c = 1 + t
        out(main_rows(c), c, w_main)
        return carry

    lax.fori_loop(0, RET_CHUNKS, out_main, 0, unroll=16)


def _retention(proj, dec, w_out, layer):
    head_slabs = RET_HEAD_DIM // SLAB

    def col_spec(col0):
        return pl.BlockSpec((head_slabs, NP, SLAB), lambda b, h: (col0 // RET_HEAD_DIM + h, b, 0))

    wr = D_MODEL // (BATCH * RET_HEADS)
    return pl.pallas_call(
        _ret_kernel,
        grid=(BATCH, RET_HEADS),
        in_specs=[
            pl.BlockSpec(memory_space=pltpu.SMEM),
            col_spec(COL_RQ), col_spec(COL_RK), col_spec(COL_RV), col_spec(COL_RG),
            pl.BlockSpec((None, wr, D_MODEL), lambda b, h: (layer, b * RET_HEADS + h, 0)),
        ],
        out_specs=[
            pl.BlockSpec((head_slabs, NP, SLAB), lambda b, h: (h, b, 0)),
            pl.BlockSpec((wr, D_MODEL), lambda b, h: (b * RET_HEADS + h, 0)),
        ],
        out_shape=[
            jax.ShapeDtypeStruct((RET_WIDTH // SLAB, M_ROWS, SLAB), BF16),
            jax.ShapeDtypeStruct((D_MODEL, D_MODEL), BF16),
        ],
        scratch_shapes=[
            pltpu.VMEM((RET_CHUNKS + 1, 2 * RET_HEAD_DIM, RET_HEAD_DIM), BF16),
            pltpu.VMEM((RET_HEAD_DIM, RET_HEAD_DIM), F32),
            pltpu.VMEM((RET_HEAD_DIM, RET_HEAD_DIM), F32),
        ],
        compiler_params=pltpu.CompilerParams(
            dimension_semantics=("arbitrary", "arbitrary"),
            vmem_limit_bytes=VMEM_LIMIT),
        name="retention",
    )(dec, proj, proj, proj, proj, w_out)


def _post_norms(y, h_rows, gpost, gnext, valid):
    hn = h_rows + _rms(y, gpost)
    scale = lax.rsqrt(jnp.mean(hn * hn, axis=-1, keepdims=True) + EPS)
    if valid is not None:
        scale = jnp.where(valid, scale, 0.0)
    return hn, (hn * scale * gnext).astype(BF16)


def _outproj_kernel(att_ref, ret_ref, w_ref, gpost_ref, gffn_ref, h_ref, ho_ref, u_ref):
    chunks = _chunk_slices(OUT_CHUNKS)
    n_chunks = len(chunks)
    rows = lambda k: chunks[k]

    def project(k):
        r = rows(k)
        mixed = jnp.concatenate(
            [att_ref[c, r, :] for c in range(ATT_WIDTH // SLAB)]
            + [ret_ref[c, r, :] for c in range(RET_WIDTH // SLAB)], axis=1)
        return jnp.dot(mixed, w_ref[...], preferred_element_type=F32)

    y = project(0)
    for k in range(n_chunks):
        y_next = project(k + 1) if k + 1 < n_chunks else None
        hn, un = _post_norms(y, h_ref[rows(k), :], gpost_ref[...], gffn_ref[...], None)
        ho_ref[rows(k), :] = hn
        u_ref[rows(k), :] = un
        y = y_next


def _outproj(att, ret, w, gpost, gffn, h):
    row = lambda i: (i, 0)
    const = lambda i: (0, 0)
    return pl.pallas_call(
        _outproj_kernel,
        grid=(M_ROWS // TM_OUT,),
        in_specs=[
            pl.BlockSpec((ATT_WIDTH // SLAB, TM_OUT, SLAB), lambda i: (0, i, 0)),
            pl.BlockSpec((RET_WIDTH // SLAB, TM_OUT, SLAB), lambda i: (0, i, 0)),
            pl.BlockSpec((D_MODEL, D_MODEL), const),
            pl.BlockSpec((1, D_MODEL), const),
            pl.BlockSpec((1, D_MODEL), const),
            pl.BlockSpec((TM_OUT, D_MODEL), row),
        ],
        out_specs=[
            pl.BlockSpec((TM_OUT, D_MODEL), row),
            pl.BlockSpec((TM_OUT, D_MODEL), row),
        ],
        out_shape=[
            jax.ShapeDtypeStruct((M_ROWS, D_MODEL), F32),
            jax.ShapeDtypeStruct((M_ROWS, D_MODEL), BF16),
        ],
        input_output_aliases={5: 0},
        compiler_params=pltpu.CompilerParams(
            dimension_semantics=("arbitrary",),
            vmem_limit_bytes=VMEM_LIMIT),
        name="outproj",
    )(att, ret, w, gpost, gffn, h)


def _ffn_steps(u_ref, wg_ref, wu_ref, wd_ref, acc_ref, finish, side_job=lambda: None):
    j = pl.program_id(1)
    last = pl.num_programs(1) - 1

    def swiglu(rows):
        u = u_ref[rows, :]
        g = jnp.dot(u, wg_ref[...], preferred_element_type=F32)
        up = jnp.dot(u, wu_ref[...], preferred_element_type=F32)
        f = (_silu(g) * up).astype(BF16)
        return jnp.dot(f, wd_ref[...], preferred_element_type=F32)

    @pl.when(j == 0)
    def _():
        side_job()
        acc_ref[...] = swiglu(slice(None))

    @pl.when((j > 0) & (j < last))
    def _():
        side_job()
        acc_ref[...] += swiglu(slice(None))

    @pl.when(j == last)
    def _():
        side_job()
        rows = _chunk_slices(FFN_CHUNKS)
        y_next = acc_ref[rows[0], :] + swiglu(rows[0])
        for k, r in enumerate(rows):
            y = y_next
            if k + 1 < len(rows):
                y_next = acc_ref[rows[k + 1], :] + swiglu(rows[k + 1])
            finish(r, y)


def _ffn_mid_kernel(layer_ref, u_ref, wg_ref, wu_ref, wd_ref, gpost_ref, gnext_ref, h_ref, win_ref,
                    ho_ref, un_ref, winb_ref):
    del layer_ref
    i = pl.program_id(0)

    def cast_next_w_in_tile():
        winb_ref[...] = win_ref[...].astype(BF16)

    def finish(r, y):
        valid = _valid_rows(i * TM_FFN + r.start, r.stop - r.start)
        hn, un = _post_norms(y, h_ref[r, :], gpost_ref[...], gnext_ref[...], valid)
        ho_ref[r, :] = hn
        un_ref[r, :] = un

    _ffn_steps(u_ref, wg_ref, wu_ref, wd_ref, ho_ref, finish, cast_next_w_in_tile)


def _ffn_mid(u, wg, wu, wd, gpost, gnext, h, w_in, next_layer):
    row = lambda i, j, l: (i, 0)
    const = lambda i, j, l: (0, 0)
    n_i = M_ROWS // TM_FFN
    n_j = D_FF // TF_FFN
    assert n_i >= IN_CAST_TILES and n_j * TN_IN == IN_COLS
    ct = D_MODEL // IN_CAST_TILES
    tile = lambda i: jnp.minimum(i, IN_CAST_TILES - 1)
    return pl.pallas_call(
        _ffn_mid_kernel,
        grid_spec=pltpu.PrefetchScalarGridSpec(
            num_scalar_prefetch=1,
            grid=(n_i, n_j),
            in_specs=[
                pl.BlockSpec((TM_FFN, D_MODEL), row),
                pl.BlockSpec((None, D_MODEL, TF_FFN), lambda i, j, l: (j, 0, 0)),
                pl.BlockSpec((None, D_MODEL, TF_FFN), lambda i, j, l: (j, 0, 0)),
                pl.BlockSpec((TF_FFN, D_MODEL), lambda i, j, l: (j, 0)),
                pl.BlockSpec((1, D_MODEL), const),
                pl.BlockSpec((1, D_MODEL), const),
                pl.BlockSpec((TM_FFN, D_MODEL), row),
                pl.BlockSpec((None, ct, TN_IN), lambda i, j, l: (l[0], tile(i), j)),
            ],
            out_specs=[
                pl.BlockSpec((TM_FFN, D_MODEL), row),
                pl.BlockSpec((TM_FFN, D_MODEL), row),
                pl.BlockSpec((None, ct, TN_IN), lambda i, j, l: (j, i, 0)),
            ],
        ),
        out_shape=[
            jax.ShapeDtypeStruct((M_ROWS, D_MODEL), F32),
            jax.ShapeDtypeStruct((M_ROWS, D_MODEL), BF16),
            jax.ShapeDtypeStruct((n_j, n_i * ct, TN_IN), BF16),
        ],
        input_output_aliases={7: 0},
        compiler_params=pltpu.CompilerParams(
            dimension_semantics=("arbitrary", "arbitrary"),
            vmem_limit_bytes=VMEM_LIMIT),
        name="ffn",
    )(next_layer, u, wg, wu, wd, gpost, gnext, h, w_in)


def _token_block_copies(tile, acc_ref, slot, out_ref, sem):
    copies = []
    for q in range(TM_FFN // BLOCK):
        gb = tile * (TM_FFN // BLOCK) + q
        b = sum((gb >= k * NB).astype(jnp.int32) for k in range(1, BATCH))
        n = gb - b * NB
        dst = pl.multiple_of((b * (SEQ // BLOCK) + jnp.maximum(n - 1, 0)) * BLOCK, BLOCK)
        copy = pltpu.make_async_copy(
            acc_ref.at[slot, pl.ds(q * BLOCK, BLOCK), :], out_ref.at[pl.ds(dst, BLOCK), :],
            sem.at[slot])
        copies.append((n >= 1, copy))
    return copies


def _ffn_last_kernel(u_ref, wg_ref, wu_ref, wd_ref, gpost_ref, h_ref, out_ref, acc_ref, sem):
    i = pl.program_id(0)
    j = pl.program_id(1)
    slot = i % 2
    acc = acc_ref.at[slot]

    def finish(r, y):
        acc[r, :] = h_ref[r, :] + _rms(y, gpost_ref[...])

    _ffn_steps(u_ref, wg_ref, wu_ref, wd_ref, acc, finish)

    @pl.when(j == pl.num_programs(1) - 1)
    def _():
        for is_token, copy in _token_block_copies(i, acc_ref, slot, out_ref, sem):
            @pl.when(is_token)
            def _():
                copy.start()

        @pl.when(i > 0)
        def _():
            for is_token, copy in _token_block_copies(i - 1, acc_ref, 1 - slot, out_ref, sem):
                @pl.when(is_token)
                def _():
                    copy.wait()

        @pl.when(i == pl.num_programs(0) - 1)
        def _():
            for is_token, copy in _token_block_copies(i, acc_ref, slot, out_ref, sem):
                @pl.when(is_token)
                def _():
                    copy.wait()


def _ffn_last(u, wg, wu, wd, gpost, h):
    row = lambda i, j: (i, 0)
    const = lambda i, j: (0, 0)
    return pl.pallas_call(
        _ffn_last_kernel,
        grid=(M_ROWS // TM_FFN, D_FF // TF_FFN),
        in_specs=[
            pl.BlockSpec((TM_FFN, D_MODEL), row),
            pl.BlockSpec((None, D_MODEL, TF_FFN), lambda i, j: (j, 0, 0)),
            pl.BlockSpec((None, D_MODEL, TF_FFN), lambda i, j: (j, 0, 0)),
            pl.BlockSpec((TF_FFN, D_MODEL), lambda i, j: (j, 0)),
            pl.BlockSpec((1, D_MODEL), const),
            pl.BlockSpec((TM_FFN, D_MODEL), row),
        ],
        out_specs=pl.BlockSpec(memory_space=pl.ANY),
        out_shape=jax.ShapeDtypeStruct((BATCH * SEQ, D_MODEL), F32),
        scratch_shapes=[
            pltpu.VMEM((2, TM_FFN, D_MODEL), F32),
            pltpu.SemaphoreType.DMA((2,)),
        ],
        compiler_params=pltpu.CompilerParams(
            dimension_semantics=("arbitrary", "arbitrary"),
            vmem_limit_bytes=VMEM_LIMIT),
        name="ffn_last",
    )(u, wg, wu, wd, gpost, h)


def _rope_tables():
    f32 = np.float32
    pos = (np.arange(NP) - PAD_FRONT).astype(f32)
    half_a = ROT_DIM // 2
    inv_a = f32(ROPE_THETA) ** (-np.arange(half_a, dtype=f32) / f32(half_a))
    ang_a = pos[:, None] * inv_a[None, :]
    cos_a, sin_a = np.cos(ang_a), np.sin(ang_a)
    rest = ATT_HEAD_DIM - ROT_DIM
    ca = np.concatenate([cos_a, cos_a, np.ones((NP, rest), f32)], axis=1)
    s1 = np.concatenate([np.zeros((NP, half_a), f32), sin_a, np.zeros((NP, rest), f32)], axis=1)
    s2 = np.concatenate([-sin_a, np.zeros((NP, ATT_HEAD_DIM - half_a), f32)], axis=1)
    half_r = RET_HEAD_DIM // 2
    inv_r = f32(RET_THETA) ** (-np.arange(half_r, dtype=f32) / f32(half_r))
    ang_r = pos[:, None] * inv_r[None, :]
    return tuple(np.asarray(t, f32) for t in (ca, s1, s2, np.cos(ang_r), np.sin(ang_r)))


def kernel(x, meta_tokens, w_in, w_out, attn_sink, ret_decay_fwd, ret_decay_bwd, ret_norm,
           norm_mix_pre, norm_mix_post, w_gate, w_up, w_down, norm_ffn_pre, norm_ffn_post):
    tabs = _rope_tables()
    row = lambda a, l: a[l].reshape(1, -1).astype(F32)
    w_in, w_out, w_gate, w_up, w_down = (
        w.astype(F32) for w in (w_in, w_out, w_gate, w_up, w_down))
    h, u, w_in_l = _embed(x, meta_tokens, row(norm_mix_pre, 0), w_in)
    for l in range(DEPTH):
        layer = jnp.full((1,), l, jnp.int32)
        proj, wg, wu, wd = _inproj(u, w_in_l, layer, tabs, row(ret_norm, l), w_gate, w_up, w_down)
        att = _attention(proj, attn_sink[l].astype(F32))
        dec = jnp.stack([ret_decay_fwd[l], ret_decay_bwd[l]]).astype(F32)
        ret, w_out_l = _retention(proj, dec, w_out, l)
        h, u = _outproj(att, ret, w_out_l, row(norm_mix_post, l), row(norm_ffn_pre, l), h)
        if l + 1 < DEPTH:
            h, u, w_in_l = _ffn_mid(u, wg, wu, wd, row(norm_ffn_post, l), row(norm_mix_pre, l + 1),
                                    h, w_in, jnp.full((1,), l + 1, jnp.int32))
        else:
            out = _ffn_last(u, wg, wu, wd, row(norm_ffn_post, l), h)
    return out.reshape(BATCH, SEQ, D_MODEL)
```
